```python
import math
import jax, jax.numpy as jnp
from jax import lax
import numpy as np

D_MODEL = 2048
BATCH = 8
SEQ = 2048
DEPTH = 1

MIX_WIDTH = D_MODEL
ATT_HEAD_DIM = 128
ATT_WIDTH = MIX_WIDTH // 2
ATT_HEADS = ATT_WIDTH // ATT_HEAD_DIM
MOBA_BLOCK = 256
MOBA_TOPK = 3
Q_BLOCK = 128
REL_BUCKETS = 32
REL_MAX_DIST = 128
SSM_WIDTH = MIX_WIDTH - ATT_WIDTH
SSM_HEAD_DIM = 64
SSM_HEADS = SSM_WIDTH // SSM_HEAD_DIM
SSM_GROUPS = 2
SSM_STATE = 128
SSM_CONV = 4
SSD_CHUNK = 128
CONV_CH = SSM_WIDTH + 2 * SSM_GROUPS * SSM_STATE
IN_PROJ = 3 * ATT_WIDTH + SSM_WIDTH + CONV_CH + SSM_HEADS
PEER_HEADS = 8
PEER_NKEYS = 128
PEER_EXPERTS = PEER_NKEYS * PEER_NKEYS
PEER_QDIM = 256
PEER_TOPK = 16
PEER_TOKEN_BLOCK = 128
EPS = 1e-6

kernel_name = 'hybrid_moba_ssd_peer_layer'


def rms_norm(x, gain):
    xf = x.astype(jnp.float32)
    y = xf * lax.rsqrt(jnp.mean(xf * xf, axis=-1, keepdims=True) + EPS)
    return (y * gain.astype(jnp.float32)).astype(x.dtype)


def t5_bucket(rel):
    n = jnp.maximum(rel, 0)
    max_exact = REL_BUCKETS // 2
    nf = jnp.maximum(n, 1).astype(jnp.float32)
    large = max_exact + (jnp.log(nf / max_exact) / math.log(REL_MAX_DIST / max_exact)
                         * (REL_BUCKETS - max_exact)).astype(jnp.int32)
    large = jnp.minimum(large, REL_BUCKETS - 1)
    return jnp.where(n < max_exact, n, large)


def moba_attention(q, k, v, rel_bias):
    bsz, s, h, hd = q.shape
    nb = -(-s // MOBA_BLOCK)
    pad = nb * MOBA_BLOCK - s
    kp = jnp.pad(k, ((0, 0), (0, pad), (0, 0), (0, 0)))
    vp = jnp.pad(v, ((0, 0), (0, pad), (0, 0), (0, 0)))
    kb = kp.reshape(bsz, nb, MOBA_BLOCK, h, hd).transpose(0, 3, 1, 2, 4)
    vb = vp.reshape(bsz, nb, MOBA_BLOCK, h, hd).transpose(0, 3, 1, 2, 4)
    k_mean = jnp.mean(kb.astype(jnp.float32), axis=3)
    gate = jnp.einsum('bshd,bhnd->bhsn', q.astype(jnp.float32), k_mean)
    n_past = jnp.arange(s) // MOBA_BLOCK
    past_mask = jnp.arange(nb)[None, :] < n_past[:, None]
    gate = jnp.where(past_mask[None, None], gate, -jnp.inf)
    k_sel = min(MOBA_TOPK, nb)
    _, sel = lax.top_k(gate, k_sel)
    qh = q.transpose(0, 2, 1, 3)
    nq = s // Q_BLOCK
    scale = hd ** -0.5
    bias_t = rel_bias.astype(jnp.float32).T
    offs = jnp.arange(MOBA_BLOCK)
    qoffs = jnp.arange(Q_BLOCK)

    def one_block(i):
        b = i // nq
        c = i % nq
        q_c = lax.dynamic_slice(qh, (b, 0, c * Q_BLOCK, 0), (1, h, Q_BLOCK, hd))[0]
        sel_c = lax.dynamic_slice(sel, (b, 0, c * Q_BLOCK, 0), (1, h, Q_BLOCK, k_sel))[0]
        kb_b = kb[b]
        vb_b = vb[b]
        kg = jax.vmap(lambda kh, ih: kh[ih])(kb_b, sel_c)
        vg = jax.vmap(lambda vh, ih: vh[ih])(vb_b, sel_c)
        own = (c * Q_BLOCK) // MOBA_BLOCK
        k_own = kb_b[:, own]
        v_own = vb_b[:, own]
        t = c * Q_BLOCK + qoffs
        rel_past = t[None, :, None, None] - (sel_c[..., None] * MOBA_BLOCK + offs)
        bias_past = jax.vmap(lambda row, bk: row[bk])(bias_t, t5_bucket(rel_past))
        rel_own = t[:, None] - (own * MOBA_BLOCK + offs)[None, :]
        bias_own = bias_t[:, t5_bucket(rel_own)]
        s_past = jnp.einsum('hqd,hqjkd->hqjk', q_c, kg).astype(jnp.float32) * scale + bias_past
        valid = (jnp.arange(k_sel) < own)[None, None, :, None]
        s_past = jnp.where(valid, s_past, -jnp.inf)
        s_own = jnp.einsum('hqd,hkd->hqk', q_c, k_own).astype(jnp.float32) * scale + bias_own
        s_own = jnp.where((rel_own >= 0)[None], s_own, -jnp.inf)
        scores = jnp.concatenate([s_past.reshape(h, Q_BLOCK, k_sel * MOBA_BLOCK), s_own], axis=-1)
        p = jax.nn.softmax(scores, axis=-1)
        p_past = p[..., :k_sel * MOBA_BLOCK].reshape(h, Q_BLOCK, k_sel, MOBA_BLOCK).astype(v.dtype)
        p_own = p[..., k_sel * MOBA_BLOCK:].astype(v.dtype)
        return (jnp.einsum('hqjk,hqjkd->hqd', p_past, vg)
                + jnp.einsum('hqk,hkd->hqd', p_own, v_own))

    o = lax.map(one_block, jnp.arange(bsz * nq))
    o = o.reshape(bsz, nq, h, Q_BLOCK, hd).transpose(0, 1, 3, 2, 4)
    return o.reshape(bsz, s, h * hd)


def causal_dwconv(u, w, b):
    ch = u.shape[-1]
    y = lax.conv_general_dilated(u, w[:, None, :].astype(u.dtype), window_strides=(1,),
                                 padding=[(SSM_CONV - 1, 0)],
                                 dimension_numbers=('NWC', 'WIO', 'NWC'),
                                 feature_group_count=ch)
    return y + b.astype(u.dtype)


def ssd(xs, dt, a, bm, cm, d_skip):
    bsz, s, h, p = xs.shape
    g, n = bm.shape[2], bm.shape[3]
    r = h // g
    L = SSD_CHUNK
    nc = s // L
    x = xs.reshape(bsz, nc, L, g, r, p)
    dtc = dt.reshape(bsz, nc, L, g, r)
    bc = bm.reshape(bsz, nc, L, g, n)
    cc = cm.reshape(bsz, nc, L, g, n)
    cs = jnp.cumsum(dtc * a.reshape(g, r), axis=2)
    seg = cs[:, :, :, None] - cs[:, :, None, :]
    causal = jnp.tril(jnp.ones((L, L), dtype=bool))[None, None, :, :, None, None]
    decay = jnp.exp(jnp.where(causal, seg, -jnp.inf))
    xdt = x * dtc[..., None]
    cb = jnp.einsum('bclgn,bcsgn->bclsg', cc, bc)
    y_diag = jnp.einsum('bclsg,bclsgr,bcsgrp->bclgrp', cb, decay, xdt)
    decay_states = jnp.exp(cs[:, :, -1:] - cs)
    states = jnp.einsum('bclgn,bclgr,bclgrp->bcgrpn', bc, decay_states, xdt)
    chunk_decay = jnp.exp(cs[:, :, -1])

    def step(hs, inp):
        st, dec = inp
        return hs * dec[..., None, None] + st, hs

    h0 = jnp.zeros((bsz, g, r, p, n), jnp.float32)
    _, prev = lax.scan(step, h0, (states.transpose(1, 0, 2, 3, 4, 5), chunk_decay.transpose(1, 0, 2, 3)))
    prev = prev.transpose(1, 0, 2, 3, 4, 5)
    y_off = jnp.einsum('bclgn,bcgrpn,bclgr->bclgrp', cc, prev, jnp.exp(cs))
    y = y_diag + y_off + x * d_skip.reshape(g, r)[..., None]
    return y.reshape(bsz, s, h * p)


def peer_ffn(hx, wq, subkeys, u_tab, v_tab):
    bsz, s, d = hx.shape
    t = bsz * s
    hf = hx.reshape(t, d)
    q = (hf @ wq).reshape(t, PEER_HEADS, 2, PEER_QDIM // 2)
    sc = jnp.einsum('thcd,hckd->thck', q, subkeys).astype(jnp.float32)
    v1, i1 = lax.top_k(sc[:, :, 0], PEER_TOPK)
    v2, i2 = lax.top_k(sc[:, :, 1], PEER_TOPK)
    cand = (v1[..., :, None] + v2[..., None, :]).reshape(t, PEER_HEADS, PEER_TOPK * PEER_TOPK)
    cidx = (i1[..., :, None] * PEER_NKEYS + i2[..., None, :]).reshape(t, PEER_HEADS, PEER_TOPK * PEER_TOPK)
    top_s, pos = lax.top_k(cand, PEER_TOPK)
    eidx = jnp.take_along_axis(cidx, pos, axis=-1)
    gates = jax.nn.softmax(top_s, axis=-1)
    hk = PEER_HEADS * PEER_TOPK
    eidx = eidx.reshape(t, hk)
    gates = gates.reshape(t, hk).astype(hx.dtype)
    nblk = t // PEER_TOKEN_BLOCK

    def blk(args):
        xb, ib, gb = args
        pre = jnp.einsum('td,tkd->tk', xb, u_tab[ib])
        act = jax.nn.gelu(pre, approximate=False) * gb
        return jnp.einsum('tk,tkd->td', act, v_tab[ib])

    out = lax.map(blk, (hf.reshape(nblk, PEER_TOKEN_BLOCK, d),
                        eidx.reshape(nblk, PEER_TOKEN_BLOCK, hk),
                        gates.reshape(nblk, PEER_TOKEN_BLOCK, hk)))
    return out.reshape(bsz, s, d)


def setup_inputs(seed: int = 0) -> dict:
    key = jax.random.key(seed)
    ks = jax.random.split(key, 18)
    f32 = jnp.float32

    def nrm(k, shape, sc):
        return jax.random.normal(k, shape, f32) * sc

    x = nrm(ks[0], (BATCH, SEQ, D_MODEL), 1.0)
    attn_norm = 1.0 + nrm(ks[1], (DEPTH, D_MODEL), 0.02)
    w_in = nrm(ks[2], (DEPTH, D_MODEL, IN_PROJ), D_MODEL ** -0.5)
    conv_w = nrm(ks[3], (DEPTH, SSM_CONV, CONV_CH), SSM_CONV ** -0.5)
    conv_b = nrm(ks[4], (DEPTH, CONV_CH), 0.01)
    dt0 = jnp.exp(jax.random.uniform(ks[5], (DEPTH, SSM_HEADS), f32)
                  * (math.log(0.1) - math.log(1e-3)) + math.log(1e-3))
    dt_bias = dt0 + jnp.log(-jnp.expm1(-dt0))
    a_log = jnp.log(jax.random.uniform(ks[6], (DEPTH, SSM_HEADS), f32, 1.0, 16.0))
    d_skip = 1.0 + nrm(ks[7], (DEPTH, SSM_HEADS), 0.1)
    rel_bias = nrm(ks[8], (REL_BUCKETS, ATT_HEADS), 0.5)
    attn_out_norm = 1.0 + nrm(ks[9], (DEPTH, ATT_WIDTH), 0.02)
    ssm_out_norm = 1.0 + nrm(ks[10], (DEPTH, SSM_WIDTH), 0.02)
    w_out = nrm(ks[11], (DEPTH, MIX_WIDTH, D_MODEL), MIX_WIDTH ** -0.5)
    ffn_norm = 1.0 + nrm(ks[12], (DEPTH, D_MODEL), 0.02)
    peer_wq = nrm(ks[13], (DEPTH, D_MODEL, PEER_HEADS * PEER_QDIM), D_MODEL ** -0.5)
    peer_subkeys = nrm(ks[14], (DEPTH, PEER_HEADS, 2, PEER_NKEYS, PEER_QDIM // 2), (PEER_QDIM // 2) ** -0.5)
    peer_u = nrm(ks[15], (DEPTH, PEER_EXPERTS, D_MODEL), D_MODEL ** -0.5)
    peer_v = nrm(ks[16], (DEPTH, PEER_EXPERTS, D_MODEL), PEER_HEADS ** -0.5)
    final_norm = 1.0 + nrm(ks[17], (D_MODEL,), 0.02)
    return {'x': x, 'attn_norm': attn_norm, 'w_in': w_in, 'conv_w': conv_w, 'conv_b': conv_b,
            'dt_bias': dt_bias, 'a_log': a_log, 'd_skip': d_skip, 'rel_bias': rel_bias,
            'attn_out_norm': attn_out_norm, 'ssm_out_norm': ssm_out_norm, 'w_out': w_out,
            'ffn_norm': ffn_norm, 'peer_wq': peer_wq, 'peer_subkeys': peer_subkeys,
            'peer_u': peer_u, 'peer_v': peer_v, 'final_norm': final_norm}


def reference(x, attn_norm, w_in, conv_w, conv_b, dt_bias, a_log, d_skip, rel_bias,
              attn_out_norm, ssm_out_norm, w_out, ffn_norm, peer_wq, peer_subkeys,
              peer_u, peer_v, final_norm):
    bsz, s, _ = x.shape
    f32 = jnp.float32
    splits = [ATT_WIDTH, 2 * ATT_WIDTH, 3 * ATT_WIDTH, 3 * ATT_WIDTH + SSM_WIDTH,
              3 * ATT_WIDTH + SSM_WIDTH + CONV_CH]
    for l in range(DEPTH):
        hn = rms_norm(x, attn_norm[l])
        proj = hn @ w_in[l]
        q, k, v, z, xbc, dt_raw = jnp.split(proj, splits, axis=-1)
        att = moba_attention(q.reshape(bsz, s, ATT_HEADS, ATT_HEAD_DIM),
                             k.reshape(bsz, s, ATT_HEADS, ATT_HEAD_DIM),
                             v.reshape(bsz, s, ATT_HEADS, ATT_HEAD_DIM), rel_bias)
        att = rms_norm(att, attn_out_norm[l])
        xbc = jax.nn.silu(causal_dwconv(xbc, conv_w[l], conv_b[l]))
        xs, b_ssm, c_ssm = jnp.split(xbc, [SSM_WIDTH, SSM_WIDTH + SSM_GROUPS * SSM_STATE], axis=-1)
        dt = jax.nn.softplus(dt_raw.astype(f32) + dt_bias[l].astype(f32))
        a = -jnp.exp(a_log[l].astype(f32))
        y = ssd(xs.reshape(bsz, s, SSM_HEADS, SSM_HEAD_DIM).astype(f32), dt, a,
                b_ssm.reshape(bsz, s, SSM_GROUPS, SSM_STATE).astype(f32),
                c_ssm.reshape(bsz, s, SSM_GROUPS, SSM_STATE).astype(f32),
                d_skip[l].astype(f32))
        y = y * jax.nn.silu(z.astype(f32))
        y = rms_norm(y.reshape(bsz, s, SSM_GROUPS, SSM_WIDTH // SSM_GROUPS),
                     ssm_out_norm[l].reshape(SSM_GROUPS, SSM_WIDTH // SSM_GROUPS))
        y = y.reshape(bsz, s, SSM_WIDTH).astype(x.dtype)
        x = x + jnp.concatenate([att, y], axis=-1) @ w_out[l]
        x = x + peer_ffn(rms_norm(x, ffn_norm[l]), peer_wq[l], peer_subkeys[l], peer_u[l], peer_v[l])
    return rms_norm(x, final_norm)
```

```python
import functools
import math

import numpy as np
import jax
import jax.numpy as jnp
from jax import lax
from jax.experimental import pallas as pl
from jax.experimental.pallas import tpu as pltpu

F32 = jnp.float32
BF16 = jnp.bfloat16
EPS = 1e-6
NEG = -1e30

LANES = 128
SUBLANES = 8
VMEM_LIMIT = 56 * 1024 * 1024

ATT_HEADS = 8
ATT_HEAD_DIM = 128
ATT_WIDTH = ATT_HEADS * ATT_HEAD_DIM
MOBA_BLOCK = 256
MOBA_TOPK = 3
REL_BUCKETS = 32
REL_MAX_DIST = 128
SSM_HEADS = 16
SSM_HEAD_DIM = 64
SSM_WIDTH = SSM_HEADS * SSM_HEAD_DIM
SSM_GROUPS = 2
SSM_STATE = 128
SSM_CONV = 4
SSD_CHUNK = 128
PEER_HEADS = 8
PEER_NKEYS = 128
PEER_HALF = 128
PEER_TOPK = 16
PEER_HK = PEER_HEADS * PEER_TOPK
PEER_TOKENS_PER_STEP = 8

HIGHEST = lax.Precision.HIGHEST


def _cparams(*sem):
    return pltpu.CompilerParams(dimension_semantics=sem, vmem_limit_bytes=VMEM_LIMIT)


def _rms(x, gain):
    return x * lax.rsqrt(jnp.mean(x * x, axis=-1, keepdims=True) + EPS) * gain


def _dot_nt(a, b, precision=None):
    return lax.dot_general(a, b, (((1,), (1,)), ((), ())), precision=precision,
                           preferred_element_type=F32)


def _norm_matmul_kernel(x_ref, g_ref, w_ref, o_ref, hn_ref):
    @pl.when(pl.program_id(1) == 0)
    def _():
        hn_ref[...] = _rms(x_ref[...], g_ref[...]).astype(BF16)

    o_ref[...] = jnp.dot(hn_ref[...], w_ref[...], preferred_element_type=F32)


def _norm_matmul(x, gain, w, tm, tn):
    t, d = x.shape
    n = w.shape[1]
    return pl.pallas_call(
        _norm_matmul_kernel,
        out_shape=jax.ShapeDtypeStruct((t, n), F32),
        grid=(t // tm, n // tn),
        in_specs=[pl.BlockSpec((tm, d), lambda i, j: (i, 0)),
                  pl.BlockSpec((1, d), lambda i, j: (0, 0)),
                  pl.BlockSpec((d, tn), lambda i, j: (0, j))],
        out_specs=pl.BlockSpec((tm, tn), lambda i, j: (i, j)),
        scratch_shapes=[pltpu.VMEM((tm, d), BF16)],
        compiler_params=_cparams("arbitrary", "arbitrary"),
        name="norm_matmul",
    )(x, gain.reshape(1, d), w)


def _in_proj_kernel(x_ref, g_ref, w_ref, wdt_ref, wdtt_ref, o_ref, dt_ref, dtt_ref, hn_ref):
    @pl.when(pl.program_id(1) == 0)
    def _():
        hn = _rms(x_ref[...], g_ref[...])
        hn_ref[...] = hn.astype(BF16)
        dt_ref[...] = jnp.dot(hn, wdt_ref[...], precision=HIGHEST, preferred_element_type=F32)
        dtt_ref[...] = _dot_nt(wdtt_ref[...], hn, precision=HIGHEST)

    o_ref[...] = jnp.dot(hn_ref[...], w_ref[...], preferred_element_type=F32)


def _in_proj(x, gain, w_main, w_dt, tm, tn):
    t, d = x.shape
    n = w_main.shape[1]
    return pl.pallas_call(
        _in_proj_kernel,
        out_shape=(jax.ShapeDtypeStruct((t, n), F32),
                   jax.ShapeDtypeStruct((t, LANES), F32),
                   jax.ShapeDtypeStruct((LANES, t), F32)),
        grid=(t // tm, n // tn),
        in_specs=[pl.BlockSpec((tm, d), lambda i, j: (i, 0)),
                  pl.BlockSpec((1, d), lambda i, j: (0, 0)),
                  pl.BlockSpec((d, tn), lambda i, j: (0, j)),
                  pl.BlockSpec((d, LANES), lambda i, j: (0, 0)),
                  pl.BlockSpec((LANES, d), lambda i, j: (0, 0))],
        out_specs=(pl.BlockSpec((tm, tn), lambda i, j: (i, j)),
                   pl.BlockSpec((tm, LANES), lambda i, j: (i, 0)),
                   pl.BlockSpec((LANES, tm), lambda i, j: (0, i))),
        scratch_shapes=[pltpu.VMEM((tm, d), BF16)],
        compiler_params=_cparams("arbitrary", "arbitrary"),
        name="in_proj",
    )(x, gain.reshape(1, d), w_main, w_dt, w_dt.T)


def _t5_bucket_np(rel):
    n = np.maximum(rel, 0)
    max_exact = REL_BUCKETS // 2
    nf = np.maximum(n, 1).astype(np.float64)
    large = max_exact + (np.log(nf / max_exact) / math.log(REL_MAX_DIST / max_exact)
                         * (REL_BUCKETS - max_exact)).astype(np.int32)
    large = np.minimum(large, REL_BUCKETS - 1)
    return np.where(n < max_exact, n, large).astype(np.int32)


def _bucket_tiles(seq):
    qi = np.arange(MOBA_BLOCK)[:, None]
    ki = np.arange(MOBA_BLOCK)[None, :]
    own = _t5_bucket_np(qi - ki)
    prev = _t5_bucket_np(MOBA_BLOCK + qi - ki)
    far = _t5_bucket_np(np.arange(MOBA_BLOCK + 1, max(seq, MOBA_BLOCK + 2)))
    assert (far == far[0]).all(), "bias must be constant beyond the previous block"
    return np.stack([own, prev]), int(far[0])


def _bias_tiles_kernel(rb_ref, bk_ref, o_ref):
    h = pl.program_id(0)
    bk = bk_ref[...]
    acc = jnp.zeros(bk.shape, F32)
    for b in range(REL_BUCKETS):
        acc = jnp.where(bk == b, rb_ref[b, h], acc)
    o_ref[0] = acc


def _bias_tiles(rel_bias, buckets):
    nh = rel_bias.shape[1]
    return pl.pallas_call(
        _bias_tiles_kernel,
        out_shape=jax.ShapeDtypeStruct((nh, 2, MOBA_BLOCK, MOBA_BLOCK), F32),
        grid=(nh,),
        in_specs=[pl.BlockSpec(memory_space=pltpu.SMEM),
                  pl.BlockSpec((2, MOBA_BLOCK, MOBA_BLOCK), lambda h: (0, 0, 0))],
        out_specs=pl.BlockSpec((1, 2, MOBA_BLOCK, MOBA_BLOCK), lambda h: (h, 0, 0, 0)),
        compiler_params=_cparams("arbitrary"),
        name="bias_tiles",
    )(rel_bias, buckets)


def _moba_kernel(far_bucket, rb_ref, q_ref, k_ref, v_ref, bias_ref, o_ref, kmean_ref):
    h = pl.program_id(1)
    c = pl.program_id(2)
    nb = kmean_ref.shape[0]
    blk = MOBA_BLOCK
    scale = ATT_HEAD_DIM ** -0.5

    @pl.when(c == 0)
    def _():
        kk = k_ref[...].reshape(nb, blk, ATT_HEAD_DIM)
        kmean_ref[...] = jnp.sum(kk, axis=1) * (1.0 / blk)

    q = q_ref[...]
    gate = _dot_nt(q, kmean_ref[...], precision=HIGHEST)
    lane = lax.broadcasted_iota(jnp.int32, gate.shape, 1)
    past = lane < c
    gm = jnp.where(past, gate, -jnp.inf)
    rank = jnp.zeros(gate.shape, jnp.int32)
    for j in range(nb):
        col = gm[:, j:j + 1]
        beats = (col > gm) | ((col == gm) & (j < lane))
        rank = rank + beats.astype(jnp.int32)
    sel = jnp.where(past & (rank < MOBA_TOPK), 1.0, 0.0)

    qb = q.astype(BF16)
    row = lax.broadcasted_iota(jnp.int32, (blk, blk), 0)
    colk = lax.broadcasted_iota(jnp.int32, (blk, blk), 1)

    k_own = k_ref[pl.ds(pl.multiple_of(c * blk, blk), blk), :].astype(BF16)
    v_own = v_ref[pl.ds(pl.multiple_of(c * blk, blk), blk), :].astype(BF16)
    s = _dot_nt(qb, k_own) * scale + bias_ref[0, 0]
    s = jnp.where(row >= colk, s, NEG)
    m0 = jnp.max(s, axis=1, keepdims=True)
    p = jnp.exp(s - m0)
    l0 = jnp.sum(p, axis=1, keepdims=True)
    acc0 = jnp.dot(p.astype(BF16), v_own, preferred_element_type=F32)
    far = rb_ref[far_bucket, h]

    def body(j, carry):
        m, l, acc = carry
        off = pl.multiple_of(j * blk, blk)
        kj = k_ref[pl.ds(off, blk), :].astype(BF16)
        vj = v_ref[pl.ds(off, blk), :].astype(BF16)
        bias = jnp.where(j == c - 1, bias_ref[0, 1], far)
        sj = _dot_nt(qb, kj) * scale + bias
        selcol = jnp.sum(jnp.where(lane == j, sel, 0.0), axis=1, keepdims=True)
        sj = jnp.where(selcol > 0.0, sj, NEG)
        m_new = jnp.maximum(m, jnp.max(sj, axis=1, keepdims=True))
        alpha = jnp.exp(m - m_new)
        pj = jnp.exp(sj - m_new)
        l_new = alpha * l + jnp.sum(pj, axis=1, keepdims=True)
        acc_new = alpha * acc + jnp.dot(pj.astype(BF16), vj, preferred_element_type=F32)
        return m_new, l_new, acc_new

    _, l, acc = lax.fori_loop(0, c, body, (m0, l0, acc0))
    o_ref[...] = acc / l


def _moba(proj, rel_bias, bias_tiles, far_bucket, bsz, seq):
    nb = seq // MOBA_BLOCK
    t = bsz * seq
    kcol = ATT_WIDTH // ATT_HEAD_DIM
    return pl.pallas_call(
        functools.partial(_moba_kernel, far_bucket),
        out_shape=jax.ShapeDtypeStruct((t, ATT_WIDTH), F32),
        grid=(bsz, ATT_HEADS, nb),
        in_specs=[pl.BlockSpec(memory_space=pltpu.SMEM),
                  pl.BlockSpec((MOBA_BLOCK, ATT_HEAD_DIM), lambda b, h, c: (b * nb + c, h)),
                  pl.BlockSpec((seq, ATT_HEAD_DIM), lambda b, h, c: (b, kcol + h)),
                  pl.BlockSpec((seq, ATT_HEAD_DIM), lambda b, h, c: (b, 2 * kcol + h)),
                  pl.BlockSpec((1, 2, MOBA_BLOCK, MOBA_BLOCK), lambda b, h, c: (h, 0, 0, 0))],
        out_specs=pl.BlockSpec((MOBA_BLOCK, ATT_HEAD_DIM), lambda b, h, c: (b * nb + c, h)),
        scratch_shapes=[pltpu.VMEM((nb, ATT_HEAD_DIM), F32)],
        compiler_params=_cparams("arbitrary", "arbitrary", "arbitrary"),
        name="moba",
    )(rel_bias, proj, proj, proj, bias_tiles)


def _silu(x):
    return x / (1.0 + jnp.exp(-x))


def _softplus(x):
    return jnp.maximum(x, 0.0) + jnp.log(1.0 + jnp.exp(-jnp.abs(x)))


def _ssd_kernel(z_ref, xs_ref, bc_ref, dt_ref, dtt_ref, cwx_ref, cwbc_ref, cbx_ref, cbbc_ref,
                dtb_ref, dtbt_ref, a_ref, at_ref, dsk_ref, gain_ref, o_ref,
                extx_ref, extbc_ref, st_ref, y_ref):
    c = pl.program_id(1)
    L = SSD_CHUNK
    P = SSM_HEAD_DIM
    N = SSM_STATE
    hpg = SSM_HEADS // SSM_GROUPS

    @pl.when(c == 0)
    def _():
        extx_ref[0:SUBLANES, :] = jnp.zeros((SUBLANES, SSM_WIDTH), F32)
        extbc_ref[0:SUBLANES, :] = jnp.zeros((SUBLANES, 2 * SSM_GROUPS * N), F32)
        st_ref[...] = jnp.zeros(st_ref.shape, F32)

    extx_ref[SUBLANES:SUBLANES + L, :] = xs_ref[...]
    extbc_ref[SUBLANES:SUBLANES + L, :] = bc_ref[...]

    def conv(ext_ref, w_ref, b_ref):
        acc = b_ref[...]
        for i in range(SSM_CONV):
            acc = acc + ext_ref[SUBLANES - i:SUBLANES - i + L, :] * w_ref[SSM_CONV - 1 - i:SSM_CONV - i, :]
        return _silu(acc)

    xs = conv(extx_ref, cwx_ref, cbx_ref)
    bcm = conv(extbc_ref, cwbc_ref, cbbc_ref)
    extx_ref[0:SUBLANES, :] = extx_ref[L:L + SUBLANES, :]
    extbc_ref[0:SUBLANES, :] = extbc_ref[L:L + SUBLANES, :]

    dt = _softplus(dt_ref[...] + dtb_ref[...])
    dtt = _softplus(dtt_ref[...] + dtbt_ref[...])
    da = dt * (-jnp.exp(a_ref[...]))
    dat = dtt * (-jnp.exp(at_ref[...]))
    r = lax.broadcasted_iota(jnp.int32, (L, L), 0)
    s_ = lax.broadcasted_iota(jnp.int32, (L, L), 1)
    causal = r >= s_
    tril = jnp.where(causal, 1.0, 0.0)
    triu = jnp.where(r <= s_, 1.0, 0.0)
    cs = jnp.dot(tril, da, precision=HIGHEST, preferred_element_type=F32)
    cst = jnp.dot(dat, triu, precision=HIGHEST, preferred_element_type=F32)
    ecs = jnp.exp(cs)
    cs_last = cs[L - 1:L, :]
    dst = jnp.exp(cs_last - cs)
    cdec = jnp.exp(cs_last)

    for g in range(SSM_GROUPS):
        bg = bcm[:, g * N:(g + 1) * N]
        cg = bcm[:, (SSM_GROUPS + g) * N:(SSM_GROUPS + g + 1) * N]
        bgb = bg.astype(BF16)
        cgb = cg.astype(BF16)
        cb = _dot_nt(cgb, bgb)
        bgt = jnp.transpose(bg).astype(BF16)
        for hh in range(hpg):
            h = g * hpg + hh
            xh = xs[:, h * P:(h + 1) * P]
            xdt = xh * dt[:, h:h + 1]
            seg = cs[:, h:h + 1] - cst[h:h + 1, :]
            decay = jnp.exp(jnp.where(causal, seg, NEG))
            y = jnp.dot((cb * decay).astype(BF16), xdt.astype(BF16), preferred_element_type=F32)
            st = st_ref[h]
            y = y + jnp.dot(cgb, st.astype(BF16), preferred_element_type=F32) * ecs[:, h:h + 1]
            y = y + xh * dsk_ref[:, h:h + 1]
            y_ref[:, h * P:(h + 1) * P] = y
            new = jnp.dot(bgt, (xdt * dst[:, h:h + 1]).astype(BF16), preferred_element_type=F32)
            st_ref[h] = st * cdec[:, h:h + 1] + new

    y = y_ref[...] * _silu(z_ref[...])
    gw = SSM_WIDTH // SSM_GROUPS
    for g in range(SSM_GROUPS):
        yg = y[:, g * gw:(g + 1) * gw]
        o_ref[:, g * gw:(g + 1) * gw] = _rms(yg, gain_ref[:, g * gw:(g + 1) * gw]).astype(o_ref.dtype)


def _pad_lanes(v):
    return jnp.zeros((1, LANES), F32).at[0, :v.shape[0]].set(v.astype(F32))


def _ssd(proj, dt, dtt, conv_w, conv_b, dt_bias, a_log, d_skip, gain, bsz, seq):
    t = bsz * seq
    nc = seq // SSD_CHUNK
    L = SSD_CHUNK
    nbc = 2 * SSM_GROUPS * SSM_STATE
    zcol = 3 * ATT_WIDTH // SSM_WIDTH
    xcol = zcol + 1
    bccol = (3 * ATT_WIDTH + 2 * SSM_WIDTH) // nbc
    dtb = _pad_lanes(dt_bias)
    alog = jnp.full((1, LANES), -jnp.inf, F32).at[0, :SSM_HEADS].set(a_log.astype(F32))
    dsk = _pad_lanes(d_skip)
    full = lambda shape: pl.BlockSpec(shape, lambda b, c: tuple(0 for _ in shape))
    return pl.pallas_call(
        _ssd_kernel,
        out_shape=jax.ShapeDtypeStruct((t, SSM_WIDTH), BF16),
        grid=(bsz, nc),
        in_specs=[pl.BlockSpec((L, SSM_WIDTH), lambda b, c: (b * nc + c, zcol)),
                  pl.BlockSpec((L, SSM_WIDTH), lambda b, c: (b * nc + c, xcol)),
                  pl.BlockSpec((L, nbc), lambda b, c: (b * nc + c, bccol)),
                  pl.BlockSpec((L, LANES), lambda b, c: (b * nc + c, 0)),
                  pl.BlockSpec((LANES, L), lambda b, c: (0, b * nc + c)),
                  full((SSM_CONV, SSM_WIDTH)), full((SSM_CONV, nbc)),
                  full((1, SSM_WIDTH)), full((1, nbc)),
                  full((1, LANES)), full((LANES, 1)), full((1, LANES)), full((LANES, 1)),
                  full((1, LANES)), full((1, SSM_WIDTH))],
        out_specs=pl.BlockSpec((L, SSM_WIDTH), lambda b, c: (b * nc + c, 0)),
        scratch_shapes=[pltpu.VMEM((L + 2 * SUBLANES, SSM_WIDTH), F32),
                        pltpu.VMEM((L + 2 * SUBLANES, nbc), F32),
                        pltpu.VMEM((SSM_HEADS, SSM_STATE, SSM_HEAD_DIM), F32),
                        pltpu.VMEM((L, SSM_WIDTH), F32)],
        compiler_params=_cparams("arbitrary", "arbitrary"),
        name="ssd",
    )(proj, proj, proj, dt, dtt,
      conv_w[:, :SSM_WIDTH], conv_w[:, SSM_WIDTH:],
      conv_b[:SSM_WIDTH].reshape(1, -1), conv_b[SSM_WIDTH:].reshape(1, -1),
      dtb, dtb.reshape(LANES, 1), alog, alog.reshape(LANES, 1), dsk, gain.reshape(1, -1))


def _out_proj_kernel(att_ref, y_ref, g_ref, x_ref, w_ref, o_ref, mix_ref):
    @pl.when(pl.program_id(1) == 0)
    def _():
        mix_ref[:, :ATT_WIDTH] = _rms(att_ref[...], g_ref[...]).astype(BF16)
        mix_ref[:, ATT_WIDTH:] = y_ref[...]

    o_ref[...] = x_ref[...] + jnp.dot(mix_ref[...], w_ref[...], preferred_element_type=F32)


def _out_proj(att, y, gain, x, w, tm, tn):
    t, d = x.shape
    k = w.shape[0]
    return pl.pallas_call(
        _out_proj_kernel,
        out_shape=jax.ShapeDtypeStruct((t, d), F32),
        grid=(t // tm, d // tn),
        in_specs=[pl.BlockSpec((tm, ATT_WIDTH), lambda i, j: (i, 0)),
                  pl.BlockSpec((tm, SSM_WIDTH), lambda i, j: (i, 0)),
                  pl.BlockSpec((1, ATT_WIDTH), lambda i, j: (0, 0)),
                  pl.BlockSpec((tm, tn), lambda i, j: (i, j)),
                  pl.BlockSpec((k, tn), lambda i, j: (0, j))],
        out_specs=pl.BlockSpec((tm, tn), lambda i, j: (i, j)),
        scratch_shapes=[pltpu.VMEM((tm, k), BF16)],
        compiler_params=_cparams("arbitrary", "arbitrary"),
        name="out_proj",
    )(att, y, gain.reshape(1, -1), x, w)


def _topk_rows(sc, k):
    n = sc.shape[0]
    iota = lax.broadcasted_iota(jnp.int32, sc.shape, 0)
    vals, idxs = [], []
    for _ in range(k):
        m = jnp.max(sc, axis=0, keepdims=True)
        idx = jnp.min(jnp.where(sc == m, iota, n), axis=0, keepdims=True)
        vals.append(m)
        idxs.append(idx)
        sc = jnp.where(iota == idx, -jnp.inf, sc)
    return vals, idxs


def _peer_topk_kernel(q_ref, sk_ref, e_ref, g_ref):
    tb = q_ref.shape[0]
    K = PEER_TOPK
    for h in range(PEER_HEADS):
        halves = []
        for c in range(2):
            col = (2 * h + c) * PEER_HALF
            sc = _dot_nt(sk_ref[h, c], q_ref[:, col:col + PEER_HALF], precision=HIGHEST)
            halves.append(_topk_rows(sc, K))
        (v1, i1), (v2, i2) = halves
        v2m = jnp.concatenate(v2, axis=0)
        i2m = jnp.concatenate(i2, axis=0)
        cand = jnp.concatenate([v1[a] + v2m for a in range(K)], axis=0)
        cidx = jnp.concatenate([i1[a] * PEER_NKEYS + i2m for a in range(K)], axis=0)
        iota = lax.broadcasted_iota(jnp.int32, cand.shape, 0)
        tops, eids = [], []
        for _ in range(K):
            m = jnp.max(cand, axis=0, keepdims=True)
            pos = jnp.min(jnp.where(cand == m, iota, K * K), axis=0, keepdims=True)
            hit = iota == pos
            tops.append(m)
            eids.append(jnp.sum(jnp.where(hit, cidx, 0), axis=0, keepdims=True))
            cand = jnp.where(hit, -jnp.inf, cand)
        top_s = jnp.concatenate(tops, axis=0)
        p = jnp.exp(top_s - tops[0])
        g_ref[h * K:(h + 1) * K, :] = p / jnp.sum(p, axis=0, keepdims=True)
        e_ref[h * K:(h + 1) * K, :] = jnp.concatenate(eids, axis=0)


def _peer_topk(q, subkeys, tb):
    t, d = q.shape
    return pl.pallas_call(
        _peer_topk_kernel,
        out_shape=(jax.ShapeDtypeStruct((PEER_HK, t), jnp.int32),
                   jax.ShapeDtypeStruct((PEER_HK, t), F32)),
        grid=(t // tb,),
        in_specs=[pl.BlockSpec((tb, d), lambda i: (i, 0)),
                  pl.BlockSpec(subkeys.shape, lambda i: (0, 0, 0, 0))],
        out_specs=(pl.BlockSpec((PEER_HK, tb), lambda i: (0, i)),
                   pl.BlockSpec((PEER_HK, tb), lambda i: (0, i))),
        compiler_params=_cparams("arbitrary"),
        name="peer_topk",
    )(q, subkeys)


def _gelu(x):
    return 0.5 * x * (1.0 + lax.erf(x * (1.0 / math.sqrt(2.0))))


def _peer_experts_kernel(e_cur_ref, e_nxt_ref, x_ref, gates_ref, fg_ref, og_ref, u_hbm, v_hbm,
                         o_ref, ubuf, vbuf, sem):
    i = pl.program_id(0)
    n = pl.num_programs(0)
    tpb = PEER_TOKENS_PER_STEP
    rows = tpb * PEER_HK
    slot = lax.rem(i, 2)

    def issue(e_ref, sl):
        for j in range(tpb):
            def body(k, carry):
                e = e_ref[0, j, k]
                r = j * PEER_HK + k
                pltpu.make_async_copy(u_hbm.at[pl.ds(e, 1), :], ubuf.at[sl, pl.ds(r, 1), :],
                                      sem.at[sl, 0]).start()
                pltpu.make_async_copy(v_hbm.at[pl.ds(e, 1), :], vbuf.at[sl, pl.ds(r, 1), :],
                                      sem.at[sl, 1]).start()
                return carry
            lax.fori_loop(0, PEER_HK, body, 0, unroll=8)

    @pl.when(i == 0)
    def _():
        issue(e_cur_ref, 0)

    @pl.when(i + 1 < n)
    def _():
        issue(e_nxt_ref, 1 - slot)

    pltpu.make_async_copy(u_hbm.at[pl.ds(0, rows), :], ubuf.at[slot], sem.at[slot, 0]).wait()
    pltpu.make_async_copy(v_hbm.at[pl.ds(0, rows), :], vbuf.at[slot], sem.at[slot, 1]).wait()

    x = x_ref[...]
    hx = _rms(x, fg_ref[...])
    gates = gates_ref[...]
    eye = (lax.broadcasted_iota(jnp.int32, (PEER_HK, PEER_HK), 0)
           == lax.broadcasted_iota(jnp.int32, (PEER_HK, PEER_HK), 1))
    outs = []
    for j in range(tpb):
        u = ubuf[slot, j * PEER_HK:(j + 1) * PEER_HK, :]
        v = vbuf[slot, j * PEER_HK:(j + 1) * PEER_HK, :]
        pre = jnp.sum(u * hx[j:j + 1, :], axis=1, keepdims=True)
        gcol = jnp.sum(jnp.where(eye, gates[j:j + 1, :], 0.0), axis=1, keepdims=True)
        act = _gelu(pre) * gcol
        outs.append(jnp.sum(v * act, axis=0, keepdims=True))
    x2 = x + jnp.concatenate(outs, axis=0)
    o_ref[...] = _rms(x2, og_ref[...])


def _peer_experts(x1, eidx, gates, ffn_gain, out_gain, u_tab, v_tab):
    t, d = x1.shape
    tpb = PEER_TOKENS_PER_STEP
    n = t // tpb
    rows = tpb * PEER_HK
    e3 = eidx.reshape(n, tpb, PEER_HK)
    smem_blk = lambda f: pl.BlockSpec((1, tpb, PEER_HK), f, memory_space=pltpu.SMEM)
    return pl.pallas_call(
        _peer_experts_kernel,
        out_shape=jax.ShapeDtypeStruct((t, d), F32),
        grid=(n,),
        in_specs=[smem_blk(lambda i: (i, 0, 0)),
                  smem_blk(lambda i: (jnp.minimum(i + 1, n - 1), 0, 0)),
                  pl.BlockSpec((tpb, d), lambda i: (i, 0)),
                  pl.BlockSpec((tpb, PEER_HK), lambda i: (i, 0)),
                  pl.BlockSpec((1, d), lambda i: (0, 0)),
                  pl.BlockSpec((1, d), lambda i: (0, 0)),
                  pl.BlockSpec(memory_space=pl.ANY),
                  pl.BlockSpec(memory_space=pl.ANY)],
        out_specs=pl.BlockSpec((tpb, d), lambda i: (i, 0)),
        scratch_shapes=[pltpu.VMEM((2, rows, d), F32),
                        pltpu.VMEM((2, rows, d), F32),
                        pltpu.SemaphoreType.DMA((2, 2))],
        compiler_params=_cparams("arbitrary"),
        name="peer_experts",
    )(e3, e3, x1, gates, ffn_gain.reshape(1, d), out_gain.reshape(1, d), u_tab, v_tab)


def _layer(x2d, bsz, seq, attn_norm, w_in, conv_w, conv_b, dt_bias, a_log, d_skip, rel_bias,
           attn_out_norm, ssm_out_norm, w_out, ffn_norm, peer_wq, peer_subkeys, peer_u, peer_v,
           out_gain, buckets, far_bucket):
    n_main = 3 * ATT_WIDTH + 2 * SSM_WIDTH + 2 * SSM_GROUPS * SSM_STATE
    w_main = w_in[:, :n_main].astype(BF16)
    w_dt = jnp.zeros((w_in.shape[0], LANES), F32).at[:, :SSM_HEADS].set(w_in[:, n_main:])
    proj, dt, dtt = _in_proj(x2d, attn_norm, w_main, w_dt, 512, 512)
    tiles = _bias_tiles(rel_bias, buckets)
    att = _moba(proj, rel_bias, tiles, far_bucket, bsz, seq)
    y = _ssd(proj, dt, dtt, conv_w, conv_b, dt_bias, a_log, d_skip, ssm_out_norm, bsz, seq)
    x1 = _out_proj(att, y, attn_out_norm, x2d, w_out.astype(BF16), 512, 1024)
    pq = _norm_matmul(x1, ffn_norm, peer_wq.astype(BF16), 512, 1024)
    e_t, g_t = _peer_topk(pq, peer_subkeys, 256)
    return _peer_experts(x1, e_t.T, g_t.T, ffn_norm, out_gain, peer_u, peer_v)


def kernel(x, attn_norm, w_in, conv_w, conv_b, dt_bias, a_log, d_skip, rel_bias, attn_out_norm,
           ssm_out_norm, w_out, ffn_norm, peer_wq, peer_subkeys, peer_u, peer_v, final_norm):
    bsz, seq, d = x.shape
    depth = w_in.shape[0]
    assert depth == 1, "the final rmsnorm is fused into the last layer's expert kernel"
    buckets_np, far_bucket = _bucket_tiles(seq)
    buckets = jnp.asarray(buckets_np)
    out = _layer(x.reshape(bsz * seq, d), bsz, seq, attn_norm[0], w_in[0], conv_w[0], conv_b[0],
                 dt_bias[0], a_log[0], d_skip[0], rel_bias, attn_out_norm[0], ssm_out_norm[0],
                 w_out[0], ffn_norm[0], peer_wq[0], peer_subkeys[0], peer_u[0], peer_v[0],
                 final_norm, buckets, far_bucket)
    return out.reshape(bsz, seq, d)
```

```python
import functools
import math

import numpy as np
import jax
import jax.numpy as jnp
from jax import lax
from jax.experimental import pallas as pl
from jax.experimental.pallas import tpu as pltpu

F32 = jnp.float32
BF16 = jnp.bfloat16
EPS = 1e-6
NEG = -1e30

LANES = 128
SUBLANES = 8
VMEM_LIMIT = 56 * 1024 * 1024

ATT_HEADS = 8
ATT_HEAD_DIM = 128
ATT_WIDTH = ATT_HEADS * ATT_HEAD_DIM
MOBA_BLOCK = 256
MOBA_TOPK = 3
REL_BUCKETS = 32
REL_MAX_DIST = 128
SSM_HEADS = 16
SSM_HEAD_DIM = 64
SSM_WIDTH = SSM_HEADS * SSM_HEAD_DIM
SSM_GROUPS = 2
SSM_STATE = 128
SSM_CONV = 4
SSD_CHUNK = 128
PEER_HEADS = 8
PEER_NKEYS = 128
PEER_HALF = 128
PEER_TOPK = 16
PEER_HK = PEER_HEADS * PEER_TOPK
PEER_TOKENS_PER_STEP = 8

HIGHEST = lax.Precision.HIGHEST


def _cparams(*sem):
    return pltpu.CompilerParams(dimension_semantics=sem, vmem_limit_bytes=VMEM_LIMIT)


def _rms(x, gain):
    return x * lax.rsqrt(jnp.mean(x * x, axis=-1, keepdims=True) + EPS) * gain


def _dot_nt(a, b, precision=None):
    return lax.dot_general(a, b, (((1,), (1,)), ((), ())), precision=precision,
                           preferred_element_type=F32)


def _norm_matmul_kernel(x_ref, g_ref, w_ref, o_ref, hn_ref):
    @pl.when(pl.program_id(1) == 0)
    def _():
        hn_ref[...] = _rms(x_ref[...], g_ref[...]).astype(BF16)

    o_ref[...] = jnp.dot(hn_ref[...], w_ref[...], preferred_element_type=F32)


def _norm_matmul(x, gain, w, tm, tn):
    t, d = x.shape
    n = w.shape[1]
    return pl.pallas_call(
        _norm_matmul_kernel,
        out_shape=jax.ShapeDtypeStruct((t, n), F32),
        grid=(t // tm, n // tn),
        in_specs=[pl.BlockSpec((tm, d), lambda i, j: (i, 0)),
                  pl.BlockSpec((1, d), lambda i, j: (0, 0)),
                  pl.BlockSpec((d, tn), lambda i, j: (0, j))],
        out_specs=pl.BlockSpec((tm, tn), lambda i, j: (i, j)),
        scratch_shapes=[pltpu.VMEM((tm, d), BF16)],
        compiler_params=_cparams("arbitrary", "arbitrary"),
        name="norm_matmul",
    )(x, gain.reshape(1, d), w)


def _in_proj_kernel(x_ref, g_ref, w_ref, wdt_ref, wdtt_ref, o_ref, dt_ref, dtt_ref, hn_ref):
    @pl.when(pl.program_id(1) == 0)
    def _():
        hn = _rms(x_ref[...], g_ref[...])
        hn_ref[...] = hn.astype(BF16)
        dt_ref[...] = jnp.dot(hn, wdt_ref[...], precision=HIGHEST, preferred_element_type=F32)
        dtt_ref[...] = _dot_nt(wdtt_ref[...], hn, precision=HIGHEST)

    o_ref[...] = jnp.dot(hn_ref[...], w_ref[...], preferred_element_type=F32)


def _in_proj(x, gain, w_main, w_dt, tm, tn):
    t, d = x.shape
    n = w_main.shape[1]
    return pl.pallas_call(
        _in_proj_kernel,
        out_shape=(jax.ShapeDtypeStruct((t, n), F32),
                   jax.ShapeDtypeStruct((t, LANES), F32),
                   jax.ShapeDtypeStruct((LANES, t), F32)),
        grid=(t // tm, n // tn),
        in_specs=[pl.BlockSpec((tm, d), lambda i, j: (i, 0)),
                  pl.BlockSpec((1, d), lambda i, j: (0, 0)),
                  pl.BlockSpec((d, tn), lambda i, j: (0, j)),
                  pl.BlockSpec((d, LANES), lambda i, j: (0, 0)),
                  pl.BlockSpec((LANES, d), lambda i, j: (0, 0))],
        out_specs=(pl.BlockSpec((tm, tn), lambda i, j: (i, j)),
                   pl.BlockSpec((tm, LANES), lambda i, j: (i, 0)),
                   pl.BlockSpec((LANES, tm), lambda i, j: (0, i))),
        scratch_shapes=[pltpu.VMEM((tm, d), BF16)],
        compiler_params=_cparams("arbitrary", "arbitrary"),
        name="in_proj",
    )(x, gain.reshape(1, d), w_main, w_dt, w_dt.T)


def _t5_bucket_np(rel):
    n = np.maximum(rel, 0)
    max_exact = REL_BUCKETS // 2
    nf = np.maximum(n, 1).astype(np.float64)
    large = max_exact + (np.log(nf / max_exact) / math.log(REL_MAX_DIST / max_exact)
                         * (REL_BUCKETS - max_exact)).astype(np.int32)
    large = np.minimum(large, REL_BUCKETS - 1)
    return np.where(n < max_exact, n, large).astype(np.int32)


def _bucket_tiles(seq):
    qi = np.arange(MOBA_BLOCK)[:, None]
    ki = np.arange(MOBA_BLOCK)[None, :]
    own = _t5_bucket_np(qi - ki)
    prev = _t5_bucket_np(MOBA_BLOCK + qi - ki)
    far = _t5_bucket_np(np.arange(MOBA_BLOCK + 1, max(seq, MOBA_BLOCK + 2)))
    assert (far == far[0]).all(), "bias must be constant beyond the previous block"
    return np.stack([own, prev]), int(far[0])


def _bias_tiles_kernel(rb_ref, bk_ref, o_ref):
    h = pl.program_id(0)
    bk = bk_ref[...]
    acc = jnp.zeros(bk.shape, F32)
    for b in range(REL_BUCKETS):
        acc = jnp.where(bk == b, rb_ref[b, h], acc)
    o_ref[0] = acc


def _bias_tiles(rel_bias, buckets):
    nh = rel_bias.shape[1]
    return pl.pallas_call(
        _bias_tiles_kernel,
        out_shape=jax.ShapeDtypeStruct((nh, 2, MOBA_BLOCK, MOBA_BLOCK), F32),
        grid=(nh,),
        in_specs=[pl.BlockSpec(memory_space=pltpu.SMEM),
                  pl.BlockSpec((2, MOBA_BLOCK, MOBA_BLOCK), lambda h: (0, 0, 0))],
        out_specs=pl.BlockSpec((1, 2, MOBA_BLOCK, MOBA_BLOCK), lambda h: (h, 0, 0, 0)),
        compiler_params=_cparams("arbitrary"),
        name="bias_tiles",
    )(rel_bias, buckets)


def _moba_kernel(far_bucket, rb_ref, q_ref, k_ref, v_ref, bias_ref, o_ref, kmean_ref):
    h = pl.program_id(1)
    c = pl.program_id(2)
    nb = kmean_ref.shape[0]
    blk = MOBA_BLOCK
    scale = ATT_HEAD_DIM ** -0.5

    @pl.when(c == 0)
    def _():
        kk = k_ref[...].reshape(nb, blk, ATT_HEAD_DIM)
        kmean_ref[...] = jnp.sum(kk, axis=1) * (1.0 / blk)

    q = q_ref[...]
    gate = _dot_nt(q, kmean_ref[...], precision=HIGHEST)
    lane = lax.broadcasted_iota(jnp.int32, gate.shape, 1)
    past = lane < c
    gm = jnp.where(past, gate, -jnp.inf)
    rank = jnp.zeros(gate.shape, jnp.int32)
    for j in range(nb):
        col = gm[:, j:j + 1]
        beats = (col > gm) | ((col == gm) & (j < lane))
        rank = rank + beats.astype(jnp.int32)
    sel = jnp.where(past & (rank < MOBA_TOPK), 1.0, 0.0)

    qb = q.astype(BF16)
    row = lax.broadcasted_iota(jnp.int32, (blk, blk), 0)
    colk = lax.broadcasted_iota(jnp.int32, (blk, blk), 1)

    k_own = k_ref[pl.ds(pl.multiple_of(c * blk, blk), blk), :].astype(BF16)
    v_own = v_ref[pl.ds(pl.multiple_of(c * blk, blk), blk), :].astype(BF16)
    s = _dot_nt(qb, k_own) * scale + bias_ref[0, 0]
    s = jnp.where(row >= colk, s, NEG)
    m0 = jnp.max(s, axis=1, keepdims=True)
    p = jnp.exp(s - m0)
    l0 = jnp.sum(p, axis=1, keepdims=True)
    acc0 = jnp.dot(p.astype(BF16), v_own, preferred_element_type=F32)
    far = rb_ref[far_bucket, h]

    def body(j, carry):
        m, l, acc = carry
        off = pl.multiple_of(j * blk, blk)
        kj = k_ref[pl.ds(off, blk), :].astype(BF16)
        vj = v_ref[pl.ds(off, blk), :].astype(BF16)
        bias = jnp.where(j == c - 1, bias_ref[0, 1], far)
        sj = _dot_nt(qb, kj) * scale + bias
        selcol = jnp.sum(jnp.where(lane == j, sel, 0.0), axis=1, keepdims=True)
        sj = jnp.where(selcol > 0.0, sj, NEG)
        m_new = jnp.maximum(m, jnp.max(sj, axis=1, keepdims=True))
        alpha = jnp.exp(m - m_new)
        pj = jnp.exp(sj - m_new)
        l_new = alpha * l + jnp.sum(pj, axis=1, keepdims=True)
        acc_new = alpha * acc + jnp.dot(pj.astype(BF16), vj, preferred_element_type=F32)
        return m_new, l_new, acc_new

    _, l, acc = lax.fori_loop(0, c, body, (m0, l0, acc0))
    o_ref[...] = acc / l


def _moba(proj, rel_bias, bias_tiles, far_bucket, bsz, seq):
    nb = seq // MOBA_BLOCK
    t = bsz * seq
    kcol = ATT_WIDTH // ATT_HEAD_DIM
    return pl.pallas_call(
        functools.partial(_moba_kernel, far_bucket),
        out_shape=jax.ShapeDtypeStruct((t, ATT_WIDTH), F32),
        grid=(bsz, ATT_HEADS, nb),
        in_specs=[pl.BlockSpec(memory_space=pltpu.SMEM),
                  pl.BlockSpec((MOBA_BLOCK, ATT_HEAD_DIM), lambda b, h, c: (b * nb + c, h)),
                  pl.BlockSpec((seq, ATT_HEAD_DIM), lambda b, h, c: (b, kcol + h)),
                  pl.BlockSpec((seq, ATT_HEAD_DIM), lambda b, h, c: (b, 2 * kcol + h)),
                  pl.BlockSpec((1, 2, MOBA_BLOCK, MOBA_BLOCK), lambda b, h, c: (h, 0, 0, 0))],
        out_specs=pl.BlockSpec((MOBA_BLOCK, ATT_HEAD_DIM), lambda b, h, c: (b * nb + c, h)),
        scratch_shapes=[pltpu.VMEM((nb, ATT_HEAD_DIM), F32)],
        compiler_params=_cparams("arbitrary", "arbitrary", "arbitrary"),
        name="moba",
    )(rel_bias, proj, proj, proj, bias_tiles)


def _silu(x):
    return x / (1.0 + jnp.exp(-x))


def _softplus(x):
    return jnp.maximum(x, 0.0) + jnp.log(1.0 + jnp.exp(-jnp.abs(x)))


def _ssd_kernel(z_ref, xs_ref, bc_ref, dt_ref, dtt_ref, cwx_ref, cwbc_ref, cbx_ref, cbbc_ref,
                dtb_ref, dtbt_ref, a_ref, at_ref, dsk_ref, gain_ref, o_ref,
                extx_ref, extbc_ref, st_ref, y_ref):
    c = pl.program_id(1)
    L = SSD_CHUNK
    P = SSM_HEAD_DIM
    N = SSM_STATE
    hpg = SSM_HEADS // SSM_GROUPS

    @pl.when(c == 0)
    def _():
        extx_ref[0:SUBLANES, :] = jnp.zeros((SUBLANES, SSM_WIDTH), F32)
        extbc_ref[0:SUBLANES, :] = jnp.zeros((SUBLANES, 2 * SSM_GROUPS * N), F32)
        st_ref[...] = jnp.zeros(st_ref.shape, F32)

    extx_ref[SUBLANES:SUBLANES + L, :] = xs_ref[...]
    extbc_ref[SUBLANES:SUBLANES + L, :] = bc_ref[...]

    def conv(ext_ref, w_ref, b_ref):
        acc = b_ref[...]
        for i in range(SSM_CONV):
            acc = acc + ext_ref[SUBLANES - i:SUBLANES - i + L, :] * w_ref[SSM_CONV - 1 - i:SSM_CONV - i, :]
        return _silu(acc)

    xs = conv(extx_ref, cwx_ref, cbx_ref)
    bcm = conv(extbc_ref, cwbc_ref, cbbc_ref)
    extx_ref[0:SUBLANES, :] = extx_ref[L:L + SUBLANES, :]
    extbc_ref[0:SUBLANES, :] = extbc_ref[L:L + SUBLANES, :]

    dt = _softplus(dt_ref[...] + dtb_ref[...])
    dtt = _softplus(dtt_ref[...] + dtbt_ref[...])
    da = dt * (-jnp.exp(a_ref[...]))
    dat = dtt * (-jnp.exp(at_ref[...]))
    r = lax.broadcasted_iota(jnp.int32, (L, L), 0)
    s_ = lax.broadcasted_iota(jnp.int32, (L, L), 1)
    causal = r >= s_
    tril = jnp.where(causal, 1.0, 0.0)
    triu = jnp.where(r <= s_, 1.0, 0.0)
    cs = jnp.dot(tril, da, precision=HIGHEST, preferred_element_type=F32)
    cst = jnp.dot(dat, triu, precision=HIGHEST, preferred_element_type=F32)
    ecs = jnp.exp(cs)
    cs_last = cs[L - 1:L, :]
    dst = jnp.exp(cs_last - cs)
    cdec = jnp.exp(cs_last)

    for g in range(SSM_GROUPS):
        bg = bcm[:, g * N:(g + 1) * N]
        cg = bcm[:, (SSM_GROUPS + g) * N:(SSM_GROUPS + g + 1) * N]
        bgb = bg.astype(BF16)
        cgb = cg.astype(BF16)
        cb = _dot_nt(cgb, bgb)
        bgt = jnp.transpose(bg).astype(BF16)
        for hh in range(hpg):
            h = g * hpg + hh
            xh = xs[:, h * P:(h + 1) * P]
            xdt = xh * dt[:, h:h + 1]
            seg = cs[:, h:h + 1] - cst[h:h + 1, :]
            decay = jnp.exp(jnp.where(causal, seg, NEG))
            y = jnp.dot((cb * decay).astype(BF16), xdt.astype(BF16), preferred_element_type=F32)
            st = st_ref[h]
            y = y + jnp.dot(cgb, st.astype(BF16), preferred_element_type=F32) * ecs[:, h:h + 1]
            y = y + xh * dsk_ref[:, h:h + 1]
            y_ref[:, h * P:(h + 1) * P] = y
            new = jnp.dot(bgt, (xdt * dst[:, h:h + 1]).astype(BF16), preferred_element_type=F32)
            st_ref[h] = st * cdec[:, h:h + 1] + new

    y = y_ref[...] * _silu(z_ref[...])
    gw = SSM_WIDTH // SSM_GROUPS
    for g in range(SSM_GROUPS):
        yg = y[:, g * gw:(g + 1) * gw]
        o_ref[:, g * gw:(g + 1) * gw] = _rms(yg, gain_ref[:, g * gw:(g + 1) * gw]).astype(o_ref.dtype)


def _pad_lanes(v):
    return jnp.zeros((1, LANES), F32).at[0, :v.shape[0]].set(v.astype(F32))


def _ssd(proj, dt, dtt, conv_w, conv_b, dt_bias, a_log, d_skip, gain, bsz, seq):
    t = bsz * seq
    nc = seq // SSD_CHUNK
    L = SSD_CHUNK
    nbc = 2 * SSM_GROUPS * SSM_STATE
    zcol = 3 * ATT_WIDTH // SSM_WIDTH
    xcol = zcol + 1
    bccol = (3 * ATT_WIDTH + 2 * SSM_WIDTH) // nbc
    dtb = _pad_lanes(dt_bias)
    alog = jnp.full((1, LANES), -jnp.inf, F32).at[0, :SSM_HEADS].set(a_log.astype(F32))
    dsk = _pad_lanes(d_skip)
    full = lambda shape: pl.BlockSpec(shape, lambda b, c: tuple(0 for _ in shape))
    return pl.pallas_call(
        _ssd_kernel,
        out_shape=jax.ShapeDtypeStruct((t, SSM_WIDTH), BF16),
        grid=(bsz, nc),
        in_specs=[pl.BlockSpec((L, SSM_WIDTH), lambda b, c: (b * nc + c, zcol)),
                  pl.BlockSpec((L, SSM_WIDTH), lambda b, c: (b * nc + c, xcol)),
                  pl.BlockSpec((L, nbc), lambda b, c: (b * nc + c, bccol)),
                  pl.BlockSpec((L, LANES), lambda b, c: (b * nc + c, 0)),
                  pl.BlockSpec((LANES, L), lambda b, c: (0, b * nc + c)),
                  full((SSM_CONV, SSM_WIDTH)), full((SSM_CONV, nbc)),
                  full((1, SSM_WIDTH)), full((1, nbc)),
                  full((1, LANES)), full((LANES, 1)), full((1, LANES)), full((LANES, 1)),
                  full((1, LANES)), full((1, SSM_WIDTH))],
        out_specs=pl.BlockSpec((L, SSM_WIDTH), lambda b, c: (b * nc + c, 0)),
        scratch_shapes=[pltpu.VMEM((L + 2 * SUBLANES, SSM_WIDTH), F32),
                        pltpu.VMEM((L + 2 * SUBLANES, nbc), F32),
                        pltpu.VMEM((SSM_HEADS, SSM_STATE, SSM_HEAD_DIM), F32),
                        pltpu.VMEM((L, SSM_WIDTH), F32)],
        compiler_params=_cparams("arbitrary", "arbitrary"),
        name="ssd",
    )(proj, proj, proj, dt, dtt,
      conv_w[:, :SSM_WIDTH], conv_w[:, SSM_WIDTH:],
      conv_b[:SSM_WIDTH].reshape(1, -1), conv_b[SSM_WIDTH:].reshape(1, -1),
      dtb, dtb.reshape(LANES, 1), alog, alog.reshape(LANES, 1), dsk, gain.reshape(1, -1))


def _out_proj_kernel(att_ref, y_ref, g_ref, x_ref, w_ref, o_ref, mix_ref):
    @pl.when(pl.program_id(1) == 0)
    def _():
        mix_ref[:, :ATT_WIDTH] = _rms(att_ref[...], g_ref[...]).astype(BF16)
        mix_ref[:, ATT_WIDTH:] = y_ref[...]

    o_ref[...] = x_ref[...] + jnp.dot(mix_ref[...], w_ref[...], preferred_element_type=F32)


def _out_proj(att, y, gain, x, w, tm, tn):
    t, d = x.shape
    k = w.shape[0]
    return pl.pallas_call(
        _out_proj_kernel,
        out_shape=jax.ShapeDtypeStruct((t, d), F32),
        grid=(t // tm, d // tn),
        in_specs=[pl.BlockSpec((tm, ATT_WIDTH), lambda i, j: (i, 0)),
                  pl.BlockSpec((tm, SSM_WIDTH), lambda i, j: (i, 0)),
                  pl.BlockSpec((1, ATT_WIDTH), lambda i, j: (0, 0)),
                  pl.BlockSpec((tm, tn), lambda i, j: (i, j)),
                  pl.BlockSpec((k, tn), lambda i, j: (0, j))],
        out_specs=pl.BlockSpec((tm, tn), lambda i, j: (i, j)),
        scratch_shapes=[pltpu.VMEM((tm, k), BF16)],
        compiler_params=_cparams("arbitrary", "arbitrary"),
        name="out_proj",
    )(att, y, gain.reshape(1, -1), x, w)


def _peer_topk_kernel(q_ref, sk_ref, e_ref, g_ref, v_ref, i_ref, top_ref):
    tb = q_ref.shape[0]
    K = PEER_TOPK
    S = SUBLANES
    key_iota = lax.broadcasted_iota(jnp.int32, (PEER_NKEYS, tb), 0)
    sub = lax.broadcasted_iota(jnp.int32, (S, tb), 0)
    for h in range(PEER_HEADS):
        for c in range(2):
            col = (2 * h + c) * PEER_HALF
            sc = _dot_nt(sk_ref[h, c], q_ref[:, col:col + PEER_HALF], precision=HIGHEST)
            for i in range(K):
                m = jnp.max(sc, axis=0, keepdims=True)
                idx = jnp.min(jnp.where(sc == m, key_iota, PEER_NKEYS), axis=0, keepdims=True)
                v_ref[c, i:i + 1, :] = m
                i_ref[c, i:i + 1, :] = idx
                sc = jnp.where(key_iota == idx, -jnp.inf, sc)
        v2lo, v2hi = v_ref[1, 0:S, :], v_ref[1, S:K, :]
        i2lo, i2hi = i_ref[1, 0:S, :], i_ref[1, S:K, :]
        cands, eids, flats = [], [], []
        for a in range(S):
            cands.append(v_ref[0, a:a + 1, :] + v2lo)
            eids.append(i_ref[0, a:a + 1, :] * PEER_NKEYS + i2lo)
            flats.append(a * K + sub)
        cands.append(v_ref[0, 0:1, :] + v2hi)
        eids.append(i_ref[0, 0:1, :] * PEER_NKEYS + i2hi)
        flats.append(S + sub)
        cands.append(v_ref[0, S:K, :] + v_ref[1, 0:1, :])
        eids.append(i_ref[0, S:K, :] * PEER_NKEYS + i_ref[1, 0:1, :])
        flats.append((S + sub) * K)
        for i in range(K):
            m = functools.reduce(jnp.maximum, cands)
            m = jnp.max(m, axis=0, keepdims=True)
            pos = functools.reduce(jnp.minimum, [jnp.where(cd == m, fl, K * K) for cd, fl in zip(cands, flats)])
            pos = jnp.min(pos, axis=0, keepdims=True)
            hits = [fl == pos for fl in flats]
            e = functools.reduce(jnp.add, [jnp.where(ht, ei, 0) for ht, ei in zip(hits, eids)])
            e_ref[h * K + i:h * K + i + 1, :] = jnp.sum(e, axis=0, keepdims=True)
            top_ref[i:i + 1, :] = m
            cands = [jnp.where(ht, -jnp.inf, cd) for ht, cd in zip(hits, cands)]
        top_s = top_ref[...]
        p = jnp.exp(top_s - top_s[0:1, :])
        g_ref[h * K:(h + 1) * K, :] = p / jnp.sum(p, axis=0, keepdims=True)


def _peer_topk(q, subkeys, tb):
    t, d = q.shape
    return pl.pallas_call(
        _peer_topk_kernel,
        out_shape=(jax.ShapeDtypeStruct((PEER_HK, t), jnp.int32),
                   jax.ShapeDtypeStruct((PEER_HK, t), F32)),
        grid=(t // tb,),
        in_specs=[pl.BlockSpec((tb, d), lambda i: (i, 0)),
                  pl.BlockSpec(subkeys.shape, lambda i: (0, 0, 0, 0))],
        out_specs=(pl.BlockSpec((PEER_HK, tb), lambda i: (0, i)),
                   pl.BlockSpec((PEER_HK, tb), lambda i: (0, i))),
        scratch_shapes=[pltpu.VMEM((2, PEER_TOPK, tb), F32),
                        pltpu.VMEM((2, PEER_TOPK, tb), jnp.int32),
                        pltpu.VMEM((PEER_TOPK, tb), F32)],
        compiler_params=_cparams("arbitrary"),
        name="peer_topk",
    )(q, subkeys)


def _gelu(x):
    return 0.5 * x * (1.0 + lax.erf(x * (1.0 / math.sqrt(2.0))))


def _peer_experts_kernel(e_cur_ref, e_nxt_ref, x_ref, gates_ref, fg_ref, og_ref, uv_hbm,
                         o_ref, buf, sem):
    i = pl.program_id(0)
    n = pl.num_programs(0)
    tpb = x_ref.shape[0]
    nd = x_ref.shape[1] // LANES
    rows = tpb * PEER_HK
    slot = lax.rem(i, 2)

    def issue(e_ref, sl):
        def body(it, carry):
            base = pl.multiple_of(it * SUBLANES, SUBLANES)
            for u in range(SUBLANES):
                e = e_ref[base + u]
                src = uv_hbm.at[pl.ds(pl.multiple_of(e * (2 * nd), 2 * nd), 2 * nd), :]
                pltpu.make_async_copy(src, buf.at[sl, :, base + u, :], sem.at[sl]).start(priority=u % 2)
            return carry
        lax.fori_loop(0, rows // SUBLANES, body, 0)

    @pl.when(i == 0)
    def _():
        issue(e_cur_ref, 0)

    @pl.when(i + 1 < n)
    def _():
        issue(e_nxt_ref, 1 - slot)

    pltpu.make_async_copy(buf.at[1 - slot], buf.at[slot], sem.at[slot]).wait()

    x = x_ref[...]
    hx = _rms(x, fg_ref[...])
    gates = gates_ref[...]
    eye = (lax.broadcasted_iota(jnp.int32, (PEER_HK, PEER_HK), 0)
           == lax.broadcasted_iota(jnp.int32, (PEER_HK, PEER_HK), 1))
    outs = []
    for j in range(tpb):
        tok = slice(j * PEER_HK, (j + 1) * PEER_HK)
        acc = jnp.zeros((PEER_HK, LANES), F32)
        for s in range(nd):
            acc = acc + buf[slot, s, tok, :] * hx[j:j + 1, s * LANES:(s + 1) * LANES]
        pre = jnp.sum(acc, axis=1, keepdims=True)
        gcol = jnp.sum(jnp.where(eye, gates[j:j + 1, :], 0.0), axis=1, keepdims=True)
        act = _gelu(pre) * gcol
        row = [jnp.sum(buf[slot, nd + s, tok, :] * act, axis=0, keepdims=True) for s in range(nd)]
        outs.append(jnp.concatenate(row, axis=1))
    x2 = x + jnp.concatenate(outs, axis=0)
    o_ref[...] = _rms(x2, og_ref[...])


def _peer_experts(x1, eidx, gates, ffn_gain, out_gain, uv_tab):
    t, d = x1.shape
    tpb = PEER_TOKENS_PER_STEP
    n = t // tpb
    rows = tpb * PEER_HK
    nd = d // LANES
    e1 = eidx.reshape(t * PEER_HK)
    smem_blk = lambda f: pl.BlockSpec((rows,), f, memory_space=pltpu.SMEM)
    return pl.pallas_call(
        _peer_experts_kernel,
        out_shape=jax.ShapeDtypeStruct((t, d), F32),
        grid=(n,),
        in_specs=[smem_blk(lambda i: (i,)),
                  smem_blk(lambda i: (jnp.minimum(i + 1, n - 1),)),
                  pl.BlockSpec((tpb, d), lambda i: (i, 0)),
                  pl.BlockSpec((tpb, PEER_HK), lambda i: (i, 0)),
                  pl.BlockSpec((1, d), lambda i: (0, 0)),
                  pl.BlockSpec((1, d), lambda i: (0, 0)),
                  pl.BlockSpec(memory_space=pl.ANY)],
        out_specs=pl.BlockSpec((tpb, d), lambda i: (i, 0)),
        scratch_shapes=[pltpu.VMEM((2, 2 * nd, rows, LANES), F32),
                        pltpu.SemaphoreType.DMA((2,))],
        compiler_params=_cparams("arbitrary"),
        name="peer_experts",
    )(e1, e1, x1, gates, ffn_gain.reshape(1, d), out_gain.reshape(1, d), uv_tab)


def _merge_expert_tables(u_tab, v_tab):
    ne, d = u_tab.shape
    nd = d // LANES
    uv = jnp.concatenate([u_tab.reshape(ne, nd, LANES), v_tab.reshape(ne, nd, LANES)], axis=1)
    return uv.reshape(ne * 2 * nd, LANES)


def _layer(x2d, bsz, seq, attn_norm, w_in, conv_w, conv_b, dt_bias, a_log, d_skip, rel_bias,
           attn_out_norm, ssm_out_norm, w_out, ffn_norm, peer_wq, peer_subkeys, peer_u, peer_v,
           out_gain, buckets, far_bucket):
    n_main = 3 * ATT_WIDTH + 2 * SSM_WIDTH + 2 * SSM_GROUPS * SSM_STATE
    w_main = w_in[:, :n_main].astype(BF16)
    w_dt = jnp.zeros((w_in.shape[0], LANES), F32).at[:, :SSM_HEADS].set(w_in[:, n_main:])
    proj, dt, dtt = _in_proj(x2d, attn_norm, w_main, w_dt, 512, 512)
    tiles = _bias_tiles(rel_bias, buckets)
    att = _moba(proj, rel_bias, tiles, far_bucket, bsz, seq)
    y = _ssd(proj, dt, dtt, conv_w, conv_b, dt_bias, a_log, d_skip, ssm_out_norm, bsz, seq)
    x1 = _out_proj(att, y, attn_out_norm, x2d, w_out.astype(BF16), 512, 1024)
    pq = _norm_matmul(x1, ffn_norm, peer_wq.astype(BF16), 512, 1024)
    e_t, g_t = _peer_topk(pq, peer_subkeys, LANES)
    uv = _merge_expert_tables(peer_u, peer_v)
    return _peer_experts(x1, e_t.T, g_t.T, ffn_norm, out_gain, uv)


def kernel(x, attn_norm, w_in, conv_w, conv_b, dt_bias, a_log, d_skip, rel_bias, attn_out_norm,
           ssm_out_norm, w_out, ffn_norm, peer_wq, peer_subkeys, peer_u, peer_v, final_norm):
    bsz, seq, d = x.shape
    depth = w_in.shape[0]
    assert depth == 1, "the final rmsnorm is fused into the last layer's expert kernel"
    buckets_np, far_bucket = _bucket_tiles(seq)
    buckets = jnp.asarray(buckets_np)
    out = _layer(x.reshape(bsz * seq, d), bsz, seq, attn_norm[0], w_in[0], conv_w[0], conv_b[0],
                 dt_bias[0], a_log[0], d_skip[0], rel_bias, attn_out_norm[0], ssm_out_norm[0],
                 w_out[0], ffn_norm[0], peer_wq[0], peer_subkeys[0], peer_u[0], peer_v[0],
                 final_norm, buckets, far_bucket)
    return out.reshape(bsz, seq, d)
```

```python
import functools
import math

import numpy as np
import jax
import jax.numpy as jnp
from jax import lax
from jax.experimental import pallas as pl
from jax.experimental.pallas import tpu as pltpu

F32 = jnp.float32
BF16 = jnp.bfloat16
EPS = 1e-6
NEG = -1e30

LANES = 128
SUBLANES = 8
VMEM_LIMIT = 56 * 1024 * 1024

ATT_HEADS = 8
ATT_HEAD_DIM = 128
ATT_WIDTH = ATT_HEADS * ATT_HEAD_DIM
MOBA_BLOCK = 256
MOBA_TOPK = 3
MOBA_HEADS_PER_STEP = 2
REL_BUCKETS = 32
REL_MAX_DIST = 128
SSM_HEADS = 16
SSM_HEAD_DIM = 64
SSM_WIDTH = SSM_HEADS * SSM_HEAD_DIM
SSM_GROUPS = 2
SSM_STATE = 128
SSM_CONV = 4
SSD_CHUNK = 128
PEER_HEADS = 8
PEER_NKEYS = 128
PEER_HALF = 128
PEER_TOPK = 16
PEER_HK = PEER_HEADS * PEER_TOPK
PEER_TOKENS_PER_STEP = 8

HIGHEST = lax.Precision.HIGHEST


def _cparams(*sem):
    return pltpu.CompilerParams(dimension_semantics=sem, vmem_limit_bytes=VMEM_LIMIT)


def _rms(x, gain):
    return x * lax.rsqrt(jnp.mean(x * x, axis=-1, keepdims=True) + EPS) * gain


def _dot_nt(a, b, precision=None):
    return lax.dot_general(a, b, (((1,), (1,)), ((), ())), precision=precision,
                           preferred_element_type=F32)


def _norm_matmul_kernel(x_ref, g_ref, w_ref, o_ref, hn_ref):
    @pl.when(pl.program_id(1) == 0)
    def _():
        hn_ref[...] = _rms(x_ref[...], g_ref[...]).astype(BF16)

    o_ref[...] = jnp.dot(hn_ref[...], w_ref[...], preferred_element_type=F32)


def _norm_matmul(x, gain, w, tm, tn):
    t, d = x.shape
    n = w.shape[1]
    return pl.pallas_call(
        _norm_matmul_kernel,
        out_shape=jax.ShapeDtypeStruct((t, n), F32),
        grid=(t // tm, n // tn),
        in_specs=[pl.BlockSpec((tm, d), lambda i, j: (i, 0)),
                  pl.BlockSpec((1, d), lambda i, j: (0, 0)),
                  pl.BlockSpec((d, tn), lambda i, j: (0, j))],
        out_specs=pl.BlockSpec((tm, tn), lambda i, j: (i, j)),
        scratch_shapes=[pltpu.VMEM((tm, d), BF16)],
        compiler_params=_cparams("arbitrary", "arbitrary"),
        name="norm_matmul",
    )(x, gain.reshape(1, d), w)


def _in_proj_kernel(x_ref, g_ref, w_ref, wdt_ref, o_ref, dt_ref, dtt_ref, hn_ref):
    @pl.when(pl.program_id(1) == 0)
    def _():
        hn = _rms(x_ref[...], g_ref[...])
        hn_ref[...] = hn.astype(BF16)
        dt = jnp.dot(hn, wdt_ref[...], precision=HIGHEST, preferred_element_type=F32)
        dt_ref[...] = dt
        dtt_ref[...] = dt.T

    o_ref[...] = jnp.dot(hn_ref[...], w_ref[...], preferred_element_type=F32)


def _in_proj(x, gain, w_main, w_dt, tm, tn):
    t, d = x.shape
    n = w_main.shape[1]
    return pl.pallas_call(
        _in_proj_kernel,
        out_shape=(jax.ShapeDtypeStruct((t, n), F32),
                   jax.ShapeDtypeStruct((t, LANES), F32),
                   jax.ShapeDtypeStruct((LANES, t), F32)),
        grid=(t // tm, n // tn),
        in_specs=[pl.BlockSpec((tm, d), lambda i, j: (i, 0)),
                  pl.BlockSpec((1, d), lambda i, j: (0, 0)),
                  pl.BlockSpec((d, tn), lambda i, j: (0, j)),
                  pl.BlockSpec((d, LANES), lambda i, j: (0, 0))],
        out_specs=(pl.BlockSpec((tm, tn), lambda i, j: (i, j)),
                   pl.BlockSpec((tm, LANES), lambda i, j: (i, 0)),
                   pl.BlockSpec((LANES, tm), lambda i, j: (0, i))),
        scratch_shapes=[pltpu.VMEM((tm, d), BF16)],
        compiler_params=_cparams("arbitrary", "arbitrary"),
        name="in_proj",
    )(x, gain.reshape(1, d), w_main, w_dt)


def _t5_bucket_np(rel):
    n = np.maximum(rel, 0)
    max_exact = REL_BUCKETS // 2
    nf = np.maximum(n, 1).astype(np.float64)
    large = max_exact + (np.log(nf / max_exact) / math.log(REL_MAX_DIST / max_exact)
                         * (REL_BUCKETS - max_exact)).astype(np.int32)
    large = np.minimum(large, REL_BUCKETS - 1)
    return np.where(n < max_exact, n, large).astype(np.int32)


def _bucket_tiles(seq):
    qi = np.arange(MOBA_BLOCK)[:, None]
    ki = np.arange(MOBA_BLOCK)[None, :]
    own = _t5_bucket_np(qi - ki)
    prev = _t5_bucket_np(MOBA_BLOCK + qi - ki)
    far = _t5_bucket_np(np.arange(MOBA_BLOCK + 1, max(seq, MOBA_BLOCK + 2)))
    assert (far == far[0]).all(), "bias must be constant beyond the previous block"
    return np.stack([own, prev]), int(far[0])


def _bias_tiles_kernel(rb_ref, bk_ref, o_ref):
    h = pl.program_id(0)
    bk = bk_ref[...]
    acc = jnp.zeros(bk.shape, F32)
    for b in range(REL_BUCKETS):
        acc = jnp.where(bk == b, rb_ref[b, h], acc)
    o_ref[0] = acc


def _bias_tiles(rel_bias, buckets):
    nh = rel_bias.shape[1]
    return pl.pallas_call(
        _bias_tiles_kernel,
        out_shape=jax.ShapeDtypeStruct((nh, 2, MOBA_BLOCK, MOBA_BLOCK), F32),
        grid=(nh,),
        in_specs=[pl.BlockSpec(memory_space=pltpu.SMEM),
                  pl.BlockSpec((2, MOBA_BLOCK, MOBA_BLOCK), lambda h: (0, 0, 0))],
        out_specs=pl.BlockSpec((1, 2, MOBA_BLOCK, MOBA_BLOCK), lambda h: (h, 0, 0, 0)),
        compiler_params=_cparams("arbitrary"),
        name="bias_tiles",
    )(rel_bias, buckets)


def _moba_kernel(far_bucket, rb_ref, q_ref, k_ref, v_ref, bias_ref, o_ref, kmean_ref, kb_ref, vb_ref):
    hg = pl.program_id(1)
    c = pl.program_id(2)
    nh = kmean_ref.shape[0]
    nb = kmean_ref.shape[1]
    blk = MOBA_BLOCK
    hd = ATT_HEAD_DIM
    scale = hd ** -0.5

    @pl.when(c == 0)
    def _():
        k = k_ref[...]
        kb_ref[...] = k.astype(BF16)
        vb_ref[...] = v_ref[...].astype(BF16)
        for hh in range(nh):
            kk = k[:, hh * hd:(hh + 1) * hd].reshape(nb, blk, hd)
            kmean_ref[hh] = jnp.sum(kk, axis=1) * (1.0 / blk)

    lane = lax.broadcasted_iota(jnp.int32, (blk, nb), 1)
    past = lane < c
    row = lax.broadcasted_iota(jnp.int32, (blk, blk), 0)
    colk = lax.broadcasted_iota(jnp.int32, (blk, blk), 1)
    own_off = pl.multiple_of(c * blk, blk)

    qbs, sels, fars, init = [], [], [], []
    for hh in range(nh):
        hs = slice(hh * hd, (hh + 1) * hd)
        q = q_ref[:, hs]
        gate = _dot_nt(q, kmean_ref[hh], precision=HIGHEST)
        gm = jnp.where(past, gate, -jnp.inf)
        rank = jnp.zeros(gate.shape, jnp.int32)
        for j in range(nb):
            col = gm[:, j:j + 1]
            beats = (col > gm) | ((col == gm) & (j < lane))
            rank = rank + beats.astype(jnp.int32)
        sels.append(jnp.where(past & (rank < MOBA_TOPK), 1.0, 0.0))
        qb = q.astype(BF16)
        qbs.append(qb)
        fars.append(rb_ref[far_bucket, hg * nh + hh])
        s = _dot_nt(qb, kb_ref[pl.ds(own_off, blk), hs]) * scale + bias_ref[hh, 0]
        s = jnp.where(row >= colk, s, NEG)
        m0 = jnp.max(s, axis=1, keepdims=True)
        p = jnp.exp(s - m0)
        l0 = jnp.sum(p, axis=1, keepdims=True)
        acc0 = jnp.dot(p.astype(BF16), vb_ref[pl.ds(own_off, blk), hs], preferred_element_type=F32)
        init.append((m0, l0, acc0))

    def body(jp, carry):
        off = pl.multiple_of(jp * (2 * blk), 2 * blk)
        out = []
        for hh in range(nh):
            hs = slice(hh * hd, (hh + 1) * hd)
            m, l, acc = carry[hh]
            sj = _dot_nt(qbs[hh], kb_ref[pl.ds(off, 2 * blk), hs]) * scale
            halves = []
            for u in range(2):
                j = 2 * jp + u
                bias = jnp.where(j == c - 1, bias_ref[hh, 1], fars[hh])
                selcol = jnp.sum(jnp.where(lane == j, sels[hh], 0.0), axis=1, keepdims=True)
                halves.append(jnp.where(selcol > 0.0, sj[:, u * blk:(u + 1) * blk] + bias, NEG))
            sj = jnp.concatenate(halves, axis=1)
            m_new = jnp.maximum(m, jnp.max(sj, axis=1, keepdims=True))
            alpha = jnp.exp(m - m_new)
            pj = jnp.exp(sj - m_new)
            l_new = alpha * l + jnp.sum(pj, axis=1, keepdims=True)
            pv = jnp.dot(pj.astype(BF16), vb_ref[pl.ds(off, 2 * blk), hs], preferred_element_type=F32)
            out.append((m_new, l_new, alpha * acc + pv))
        return tuple(out)

    final = lax.fori_loop(0, (c + 1) // 2, body, tuple(init))
    for hh in range(nh):
        _, l, acc = final[hh]
        o_ref[:, hh * hd:(hh + 1) * hd] = acc / l


def _moba(proj, rel_bias, bias_tiles, far_bucket, bsz, seq):
    nb = seq // MOBA_BLOCK
    t = bsz * seq
    nh = MOBA_HEADS_PER_STEP
    gw = nh * ATT_HEAD_DIM
    ng = ATT_HEADS // nh
    return pl.pallas_call(
        functools.partial(_moba_kernel, far_bucket),
        out_shape=jax.ShapeDtypeStruct((t, ATT_WIDTH), F32),
        grid=(bsz, ng, nb),
        in_specs=[pl.BlockSpec(memory_space=pltpu.SMEM),
                  pl.BlockSpec((MOBA_BLOCK, gw), lambda b, g, c: (b * nb + c, g)),
                  pl.BlockSpec((seq, gw), lambda b, g, c: (b, ng + g)),
                  pl.BlockSpec((seq, gw), lambda b, g, c: (b, 2 * ng + g)),
                  pl.BlockSpec((nh, 2, MOBA_BLOCK, MOBA_BLOCK), lambda b, g, c: (g, 0, 0, 0))],
        out_specs=pl.BlockSpec((MOBA_BLOCK, gw), lambda b, g, c: (b * nb + c, g)),
        scratch_shapes=[pltpu.VMEM((nh, nb, ATT_HEAD_DIM), F32),
                        pltpu.VMEM((seq, gw), BF16),
                        pltpu.VMEM((seq, gw), BF16)],
        compiler_params=_cparams("arbitrary", "arbitrary", "arbitrary"),
        name="moba",
    )(rel_bias, proj, proj, proj, bias_tiles)


def _silu(x):
    return x / (1.0 + jnp.exp(-x))


def _softplus(x):
    return jnp.maximum(x, 0.0) + jnp.log(1.0 + jnp.exp(-jnp.abs(x)))


def _ssd_kernel(z_ref, xs_ref, bc_ref, dt_ref, dtt_ref, cwx_ref, cwbc_ref, cbx_ref, cbbc_ref,
                dtb_ref, dtbt_ref, a_ref, at_ref, dsk_ref, gain_ref, o_ref,
                extx_ref, extbc_ref, st_ref, y_ref):
    c = pl.program_id(1)
    L = SSD_CHUNK
    P = SSM_HEAD_DIM
    N = SSM_STATE
    hpg = SSM_HEADS // SSM_GROUPS

    @pl.when(c == 0)
    def _():
        extx_ref[0:SUBLANES, :] = jnp.zeros((SUBLANES, SSM_WIDTH), F32)
        extbc_ref[0:SUBLANES, :] = jnp.zeros((SUBLANES, 2 * SSM_GROUPS * N), F32)
        st_ref[...] = jnp.zeros(st_ref.shape, F32)

    extx_ref[SUBLANES:SUBLANES + L, :] = xs_ref[...]
    extbc_ref[SUBLANES:SUBLANES + L, :] = bc_ref[...]

    def conv(ext_ref, w_ref, b_ref):
        acc = b_ref[...]
        for i in range(SSM_CONV):
            acc = acc + ext_ref[SUBLANES - i:SUBLANES - i + L, :] * w_ref[SSM_CONV - 1 - i:SSM_CONV - i, :]
        return _silu(acc)

    xs = conv(extx_ref, cwx_ref, cbx_ref)
    bcm = conv(extbc_ref, cwbc_ref, cbbc_ref)
    extx_ref[0:SUBLANES, :] = extx_ref[L:L + SUBLANES, :]
    extbc_ref[0:SUBLANES, :] = extbc_ref[L:L + SUBLANES, :]

    dt = _softplus(dt_ref[...] + dtb_ref[...])
    dtt = _softplus(dtt_ref[...] + dtbt_ref[...])
    da = dt * (-jnp.exp(a_ref[...]))
    dat = dtt * (-jnp.exp(at_ref[...]))
    r = lax.broadcasted_iota(jnp.int32, (L, L), 0)
    s_ = lax.broadcasted_iota(jnp.int32, (L, L), 1)
    causal = r >= s_
    tril = jnp.where(causal, 1.0, 0.0)
    triu = jnp.where(r <= s_, 1.0, 0.0)
    cs = jnp.dot(tril, da, precision=HIGHEST, preferred_element_type=F32)
    cst = jnp.dot(dat, triu, precision=HIGHEST, preferred_element_type=F32)
    ecs = jnp.exp(cs)
    cs_last = cs[L - 1:L, :]
    dst = jnp.exp(cs_last - cs)
    cdec = jnp.exp(cs_last)

    for g in range(SSM_GROUPS):
        bg = bcm[:, g * N:(g + 1) * N]
        cg = bcm[:, (SSM_GROUPS + g) * N:(SSM_GROUPS + g + 1) * N]
        bgb = bg.astype(BF16)
        cgb = cg.astype(BF16)
        cb = _dot_nt(cgb, bgb)
        bgt = jnp.transpose(bg).astype(BF16)
        for hh in range(hpg):
            h = g * hpg + hh
            xh = xs[:, h * P:(h + 1) * P]
            xdt = xh * dt[:, h:h + 1]
            seg = cs[:, h:h + 1] - cst[h:h + 1, :]
            decay = jnp.exp(jnp.where(causal, seg, NEG))
            y = jnp.dot((cb * decay).astype(BF16), xdt.astype(BF16), preferred_element_type=F32)
            st = st_ref[h]
            y = y + jnp.dot(cgb, st.astype(BF16), preferred_element_type=F32) * ecs[:, h:h + 1]
            y = y + xh * dsk_ref[:, h:h + 1]
            y_ref[:, h * P:(h + 1) * P] = y
            new = jnp.dot(bgt, (xdt * dst[:, h:h + 1]).astype(BF16), preferred_element_type=F32)
            st_ref[h] = st * cdec[:, h:h + 1] + new

    y = y_ref[...] * _silu(z_ref[...])
    gw = SSM_WIDTH // SSM_GROUPS
    for g in range(SSM_GROUPS):
        yg = y[:, g * gw:(g + 1) * gw]
        o_ref[:, g * gw:(g + 1) * gw] = _rms(yg, gain_ref[:, g * gw:(g + 1) * gw]).astype(o_ref.dtype)


def _pad_lanes(v):
    return jnp.zeros((1, LANES), F32).at[0, :v.shape[0]].set(v.astype(F32))


def _ssd(proj, dt, dtt, conv_w, conv_b, dt_bias, a_log, d_skip, gain, bsz, seq):
    t = bsz * seq
    nc = seq // SSD_CHUNK
    L = SSD_CHUNK
    nbc = 2 * SSM_GROUPS * SSM_STATE
    zcol = 3 * ATT_WIDTH // SSM_WIDTH
    xcol = zcol + 1
    bccol = (3 * ATT_WIDTH + 2 * SSM_WIDTH) // nbc
    dtb = _pad_lanes(dt_bias)
    alog = jnp.full((1, LANES), -jnp.inf, F32).at[0, :SSM_HEADS].set(a_log.astype(F32))
    dsk = _pad_lanes(d_skip)
    full = lambda shape: pl.BlockSpec(shape, lambda b, c: tuple(0 for _ in shape))
    return pl.pallas_call(
        _ssd_kernel,
        out_shape=jax.ShapeDtypeStruct((t, SSM_WIDTH), BF16),
        grid=(bsz, nc),
        in_specs=[pl.BlockSpec((L, SSM_WIDTH), lambda b, c: (b * nc + c, zcol)),
                  pl.BlockSpec((L, SSM_WIDTH), lambda b, c: (b * nc + c, xcol)),
                  pl.BlockSpec((L, nbc), lambda b, c: (b * nc + c, bccol)),
                  pl.BlockSpec((L, LANES), lambda b, c: (b * nc + c, 0)),
                  pl.BlockSpec((LANES, L), lambda b, c: (0, b * nc + c)),
                  full((SSM_CONV, SSM_WIDTH)), full((SSM_CONV, nbc)),
                  full((1, SSM_WIDTH)), full((1, nbc)),
                  full((1, LANES)), full((LANES, 1)), full((1, LANES)), full((LANES, 1)),
                  full((1, LANES)), full((1, SSM_WIDTH))],
        out_specs=pl.BlockSpec((L, SSM_WIDTH), lambda b, c: (b * nc + c, 0)),
        scratch_shapes=[pltpu.VMEM((L + 2 * SUBLANES, SSM_WIDTH), F32),
                        pltpu.VMEM((L + 2 * SUBLANES, nbc), F32),
                        pltpu.VMEM((SSM_HEADS, SSM_STATE, SSM_HEAD_DIM), F32),
                        pltpu.VMEM((L, SSM_WIDTH), F32)],
        compiler_params=_cparams("arbitrary", "arbitrary"),
        name="ssd",
    )(proj, proj, proj, dt, dtt,
      conv_w[:, :SSM_WIDTH], conv_w[:, SSM_WIDTH:],
      conv_b[:SSM_WIDTH].reshape(1, -1), conv_b[SSM_WIDTH:].reshape(1, -1),
      dtb, dtb.reshape(LANES, 1), alog, alog.reshape(LANES, 1), dsk, gain.reshape(1, -1))


def _out_proj_kernel(att_ref, y_ref, g_ref, x_ref, w_ref, o_ref, mix_ref):
    @pl.when(pl.program_id(1) == 0)
    def _():
        mix_ref[:, :ATT_WIDTH] = _rms(att_ref[...], g_ref[...]).astype(BF16)
        mix_ref[:, ATT_WIDTH:] = y_ref[...]

    o_ref[...] = x_ref[...] + jnp.dot(mix_ref[...], w_ref[...], preferred_element_type=F32)


def _out_proj(att, y, gain, x, w, tm, tn):
    t, d = x.shape
    k = w.shape[0]
    return pl.pallas_call(
        _out_proj_kernel,
        out_shape=jax.ShapeDtypeStruct((t, d), F32),
        grid=(t // tm, d // tn),
        in_specs=[pl.BlockSpec((tm, ATT_WIDTH), lambda i, j: (i, 0)),
                  pl.BlockSpec((tm, SSM_WIDTH), lambda i, j: (i, 0)),
                  pl.BlockSpec((1, ATT_WIDTH), lambda i, j: (0, 0)),
                  pl.BlockSpec((tm, tn), lambda i, j: (i, j)),
                  pl.BlockSpec((k, tn), lambda i, j: (0, j))],
        out_specs=pl.BlockSpec((tm, tn), lambda i, j: (i, j)),
        scratch_shapes=[pltpu.VMEM((tm, k), BF16)],
        compiler_params=_cparams("arbitrary", "arbitrary"),
        name="out_proj",
    )(att, y, gain.reshape(1, -1), x, w)


def _peer_topk_kernel(q_ref, sk_ref, e_ref, g_ref, v_ref, i_ref, top_ref):
    tb = q_ref.shape[0]
    K = PEER_TOPK
    S = SUBLANES
    key_iota = lax.broadcasted_iota(jnp.int32, (PEER_NKEYS, tb), 0)
    sub = lax.broadcasted_iota(jnp.int32, (S, tb), 0)
    for h in range(PEER_HEADS):
        for c in range(2):
            col = (2 * h + c) * PEER_HALF
            sc = _dot_nt(sk_ref[h, c], q_ref[:, col:col + PEER_HALF], precision=HIGHEST)
            for i in range(K):
                m = jnp.max(sc, axis=0, keepdims=True)
                idx = jnp.min(jnp.where(sc == m, key_iota, PEER_NKEYS), axis=0, keepdims=True)
                v_ref[c, i:i + 1, :] = m
                i_ref[c, i:i + 1, :] = idx
                sc = jnp.where(key_iota == idx, -jnp.inf, sc)
        v2lo, v2hi = v_ref[1, 0:S, :], v_ref[1, S:K, :]
        i2lo, i2hi = i_ref[1, 0:S, :], i_ref[1, S:K, :]
        cands, eids, flats = [], [], []
        for a in range(S):
            cands.append(v_ref[0, a:a + 1, :] + v2lo)
            eids.append(i_ref[0, a:a + 1, :] * PEER_NKEYS + i2lo)
            flats.append(a * K + sub)
        cands.append(v_ref[0, 0:1, :] + v2hi)
        eids.append(i_ref[0, 0:1, :] * PEER_NKEYS + i2hi)
        flats.append(S + sub)
        cands.append(v_ref[0, S:K, :] + v_ref[1, 0:1, :])
        eids.append(i_ref[0, S:K, :] * PEER_NKEYS + i_ref[1, 0:1, :])
        flats.append((S + sub) * K)
        for i in range(K):
            m = functools.reduce(jnp.maximum, cands)
            m = jnp.max(m, axis=0, keepdims=True)
            pos = functools.reduce(jnp.minimum, [jnp.where(cd == m, fl, K * K) for cd, fl in zip(cands, flats)])
            pos = jnp.min(pos, axis=0, keepdims=True)
            hits = [fl == pos for fl in flats]
            e = functools.reduce(jnp.add, [jnp.where(ht, ei, 0) for ht, ei in zip(hits, eids)])
            e_ref[h * K + i:h * K + i + 1, :] = jnp.sum(e, axis=0, keepdims=True)
            top_ref[i:i + 1, :] = m
            cands = [jnp.where(ht, -jnp.inf, cd) for ht, cd in zip(hits, cands)]
        top_s = top_ref[...]
        p = jnp.exp(top_s - top_s[0:1, :])
        g_ref[h * K:(h + 1) * K, :] = p / jnp.sum(p, axis=0, keepdims=True)


def _peer_topk(q, subkeys, tb):
    t, d = q.shape
    return pl.pallas_call(
        _peer_topk_kernel,
        out_shape=(jax.ShapeDtypeStruct((PEER_HK, t), jnp.int32),
                   jax.ShapeDtypeStruct((PEER_HK, t), F32)),
        grid=(t // tb,),
        in_specs=[pl.BlockSpec((tb, d), lambda i: (i, 0)),
                  pl.BlockSpec(subkeys.shape, lambda i: (0, 0, 0, 0))],
        out_specs=(pl.BlockSpec((PEER_HK, tb), lambda i: (0, i)),
                   pl.BlockSpec((PEER_HK, tb), lambda i: (0, i))),
        scratch_shapes=[pltpu.VMEM((2, PEER_TOPK, tb), F32),
                        pltpu.VMEM((2, PEER_TOPK, tb), jnp.int32),
                        pltpu.VMEM((PEER_TOPK, tb), F32)],
        compiler_params=_cparams("arbitrary"),
        name="peer_topk",
    )(q, subkeys)


def _gelu(x):
    return 0.5 * x * (1.0 + lax.erf(x * (1.0 / math.sqrt(2.0))))


def _unpack_bf16_pair(w):
    lo = lax.bitcast_convert_type(w << 16, F32)
    hi = lax.bitcast_convert_type(w & jnp.uint32(0xFFFF0000), F32)
    return lo, hi


def _peer_experts_kernel(e_cur_ref, e_nxt_ref, x_ref, gates_ref, fg_ref, og_ref, uv_hbm,
                         o_ref, buf0, buf1, sem):
    i = pl.program_id(0)
    n = pl.num_programs(0)
    tpb = x_ref.shape[0]
    nw = x_ref.shape[1] // (2 * LANES)
    rows = tpb * PEER_HK
    bufs = (buf0, buf1)

    def issue_token(e_ref, j, dst, dst_sem):
        for k in range(PEER_HK):
            r = j * PEER_HK + k
            e = e_ref[r]
            src = uv_hbm.at[pl.ds(pl.multiple_of(e * (2 * nw), 2 * nw), 2 * nw), :]
            pltpu.make_async_copy(src, dst.at[:, r, :], dst_sem).start(priority=k % 2)

    def wait_all(b):
        pltpu.make_async_copy(bufs[1 - b], bufs[b], sem.at[b]).wait()

    @pl.when(i == 0)
    def _():
        for j in range(tpb):
            issue_token(e_cur_ref, j, buf0, sem.at[0])

    def step(b):
        cur, nxt = bufs[b], bufs[1 - b]
        wait_all(b)
        x = x_ref[...]
        hx = _rms(x, fg_ref[...])
        gates = gates_ref[...]
        eye = (lax.broadcasted_iota(jnp.int32, (PEER_HK, PEER_HK), 0)
               == lax.broadcasted_iota(jnp.int32, (PEER_HK, PEER_HK), 1))
        outs = []
        for j in range(tpb):
            tok = slice(j * PEER_HK, (j + 1) * PEER_HK)
            acc = jnp.zeros((PEER_HK, LANES), F32)
            for s in range(nw):
                lo, hi = _unpack_bf16_pair(cur[s, tok, :])
                acc = acc + lo * hx[j:j + 1, s * LANES:(s + 1) * LANES]
                acc = acc + hi * hx[j:j + 1, (nw + s) * LANES:(nw + s + 1) * LANES]
            pre = jnp.sum(acc, axis=1, keepdims=True)
            gcol = jnp.sum(jnp.where(eye, gates[j:j + 1, :], 0.0), axis=1, keepdims=True)
            act = _gelu(pre) * gcol
            los, his = [], []
            for s in range(nw):
                lo, hi = _unpack_bf16_pair(cur[nw + s, tok, :])
                los.append(jnp.sum(lo * act, axis=0, keepdims=True))
                his.append(jnp.sum(hi * act, axis=0, keepdims=True))
            outs.append(jnp.concatenate(los + his, axis=1))
            issue_token(e_nxt_ref, j, nxt, sem.at[1 - b])
        x2 = x + jnp.concatenate(outs, axis=0)
        o_ref[...] = _rms(x2, og_ref[...])

        @pl.when(i == n - 1)
        def _():
            wait_all(1 - b)

    for b in range(2):
        pl.when(lax.rem(i, 2) == b)(functools.partial(step, b))


def _peer_experts(x1, eidx, gates, ffn_gain, out_gain, uv_tab):
    t, d = x1.shape
    tpb = PEER_TOKENS_PER_STEP
    n = t // tpb
    rows = tpb * PEER_HK
    nw = d // (2 * LANES)
    e1 = eidx.reshape(t * PEER_HK)
    smem_blk = lambda f: pl.BlockSpec((rows,), f, memory_space=pltpu.SMEM)
    return pl.pallas_call(
        _peer_experts_kernel,
        out_shape=jax.ShapeDtypeStruct((t, d), F32),
        grid=(n,),
        in_specs=[smem_blk(lambda i: (i,)),
                  smem_blk(lambda i: (jnp.minimum(i + 1, n - 1),)),
                  pl.BlockSpec((tpb, d), lambda i: (i, 0)),
                  pl.BlockSpec((tpb, PEER_HK), lambda i: (i, 0)),
                  pl.BlockSpec((1, d), lambda i: (0, 0)),
                  pl.BlockSpec((1, d), lambda i: (0, 0)),
                  pl.BlockSpec(memory_space=pl.ANY)],
        out_specs=pl.BlockSpec((tpb, d), lambda i: (i, 0)),
        scratch_shapes=[pltpu.VMEM((2 * nw, rows, LANES), jnp.uint32),
                        pltpu.VMEM((2 * nw, rows, LANES), jnp.uint32),
                        pltpu.SemaphoreType.DMA((2,))],
        compiler_params=_cparams("arbitrary"),
        name="peer_experts",
    )(e1, e1, x1, gates, ffn_gain.reshape(1, d), out_gain.reshape(1, d), uv_tab)


def _pack_bf16_rows(tab):
    ne, d = tab.shape
    bits = lax.bitcast_convert_type(tab.astype(BF16), jnp.uint16).astype(jnp.uint32)
    bits = bits.reshape(ne, 2, d // (2 * LANES), LANES)
    return bits[:, 0] | (bits[:, 1] << 16)


def _merge_expert_tables(u_tab, v_tab):
    uv = jnp.concatenate([_pack_bf16_rows(u_tab), _pack_bf16_rows(v_tab)], axis=1)
    return uv.reshape(-1, LANES)


def _layer(x2d, bsz, seq, attn_norm, w_in, conv_w, conv_b, dt_bias, a_log, d_skip, rel_bias,
           attn_out_norm, ssm_out_norm, w_out, ffn_norm, peer_wq, peer_subkeys, peer_u, peer_v,
           out_gain, buckets, far_bucket):
    n_main = 3 * ATT_WIDTH + 2 * SSM_WIDTH + 2 * SSM_GROUPS * SSM_STATE
    w_main = w_in[:, :n_main].astype(BF16)
    w_dt = jnp.zeros((w_in.shape[0], LANES), F32).at[:, :SSM_HEADS].set(w_in[:, n_main:])
    proj, dt, dtt = _in_proj(x2d, attn_norm, w_main, w_dt, 512, 512)
    tiles = _bias_tiles(rel_bias, buckets)
    att = _moba(proj, rel_bias, tiles, far_bucket, bsz, seq)
    y = _ssd(proj, dt, dtt, conv_w, conv_b, dt_bias, a_log, d_skip, ssm_out_norm, bsz, seq)
    x1 = _out_proj(att, y, attn_out_norm, x2d, w_out.astype(BF16), 512, 1024)
    pq = _norm_matmul(x1, ffn_norm, peer_wq.astype(BF16), 512, 1024)
    e_t, g_t = _peer_topk(pq, peer_subkeys, LANES)
    uv = _merge_expert_tables(peer_u, peer_v)
    return _peer_experts(x1, e_t.T, g_t.T, ffn_norm, out_gain, uv)


def kernel(x, attn_norm, w_in, conv_w, conv_b, dt_bias, a_log, d_skip, rel_bias, attn_out_norm,
           ssm_out_norm, w_out, ffn_norm, peer_wq, peer_subkeys, peer_u, peer_v, final_norm):
    bsz, seq, d = x.shape
    depth = w_in.shape[0]
    assert depth == 1, "the final rmsnorm is fused into the last layer's expert kernel"
    buckets_np, far_bucket = _bucket_tiles(seq)
    buckets = jnp.asarray(buckets_np)
    out = _layer(x.reshape(bsz * seq, d), bsz, seq, attn_norm[0], w_in[0], conv_w[0], conv_b[0],
                 dt_bias[0], a_log[0], d_skip[0], rel_bias, attn_out_norm[0], ssm_out_norm[0],
                 w_out[0], ffn_norm[0], peer_wq[0], peer_subkeys[0], peer_u[0], peer_v[0],
                 final_norm, buckets, far_bucket)
    return out.reshape(bsz, seq, d)
```

```python
import functools
import math

import numpy as np
import jax
import jax.numpy as jnp
from jax import lax
from jax.experimental import pallas as pl
from jax.experimental.pallas import tpu as pltpu

F32 = jnp.float32
BF16 = jnp.bfloat16
EPS = 1e-6
NEG = -1e30

LANES = 128
SUBLANES = 8
VMEM_LIMIT = 56 * 1024 * 1024

ATT_HEADS = 8
ATT_HEAD_DIM = 128
ATT_WIDTH = ATT_HEADS * ATT_HEAD_DIM
MOBA_BLOCK = 256
MOBA_TOPK = 3
MOBA_HEADS_PER_STEP = 2
REL_BUCKETS = 32
REL_MAX_DIST = 128
SSM_HEADS = 16
SSM_HEAD_DIM = 64
SSM_WIDTH = SSM_HEADS * SSM_HEAD_DIM
SSM_GROUPS = 2
SSM_STATE = 128
SSM_CONV = 4
SSD_CHUNK = 128
PEER_HEADS = 8
PEER_NKEYS = 128
PEER_HALF = 128
PEER_TOPK = 16
PEER_HK = PEER_HEADS * PEER_TOPK
PEER_TOKENS_PER_STEP = 8
PEER_ROW_PITCH = 24

HIGHEST = lax.Precision.HIGHEST


def _cparams(*sem):
    return pltpu.CompilerParams(dimension_semantics=sem, vmem_limit_bytes=VMEM_LIMIT)


def _rms(x, gain):
    return x * lax.rsqrt(jnp.mean(x * x, axis=-1, keepdims=True) + EPS) * gain


def _dot_nt(a, b, precision=None):
    return lax.dot_general(a, b, (((1,), (1,)), ((), ())), precision=precision,
                           preferred_element_type=F32)


def _norm_matmul_kernel(x_ref, g_ref, w_ref, o_ref, hn_ref):
    @pl.when(pl.program_id(1) == 0)
    def _():
        hn_ref[...] = _rms(x_ref[...], g_ref[...]).astype(BF16)

    o_ref[...] = jnp.dot(hn_ref[...], w_ref[...], preferred_element_type=F32)


def _norm_matmul(x, gain, w, tm, tn):
    t, d = x.shape
    n = w.shape[1]
    return pl.pallas_call(
        _norm_matmul_kernel,
        out_shape=jax.ShapeDtypeStruct((t, n), F32),
        grid=(t // tm, n // tn),
        in_specs=[pl.BlockSpec((tm, d), lambda i, j: (i, 0)),
                  pl.BlockSpec((1, d), lambda i, j: (0, 0)),
                  pl.BlockSpec((d, tn), lambda i, j: (0, j))],
        out_specs=pl.BlockSpec((tm, tn), lambda i, j: (i, j)),
        scratch_shapes=[pltpu.VMEM((tm, d), BF16)],
        compiler_params=_cparams("arbitrary", "arbitrary"),
        name="norm_matmul",
    )(x, gain.reshape(1, d), w)


def _in_proj_kernel(x_ref, g_ref, w_ref, wdt_ref, o_ref, dt_ref, dtt_ref, hn_ref):
    @pl.when(pl.program_id(1) == 0)
    def _():
        hn = _rms(x_ref[...], g_ref[...])
        hn_ref[...] = hn.astype(BF16)
        dt = jnp.dot(hn, wdt_ref[...], precision=HIGHEST, preferred_element_type=F32)
        dt_ref[...] = dt
        dtt_ref[...] = dt.T

    o_ref[...] = jnp.dot(hn_ref[...], w_ref[...], preferred_element_type=F32)


def _in_proj(x, gain, w_main, w_dt, tm, tn):
    t, d = x.shape
    n = w_main.shape[1]
    return pl.pallas_call(
        _in_proj_kernel,
        out_shape=(jax.ShapeDtypeStruct((t, n), F32),
                   jax.ShapeDtypeStruct((t, LANES), F32),
                   jax.ShapeDtypeStruct((LANES, t), F32)),
        grid=(t // tm, n // tn),
        in_specs=[pl.BlockSpec((tm, d), lambda i, j: (i, 0)),
                  pl.BlockSpec((1, d), lambda i, j: (0, 0)),
                  pl.BlockSpec((d, tn), lambda i, j: (0, j)),
                  pl.BlockSpec((d, LANES), lambda i, j: (0, 0))],
        out_specs=(pl.BlockSpec((tm, tn), lambda i, j: (i, j)),
                   pl.BlockSpec((tm, LANES), lambda i, j: (i, 0)),
                   pl.BlockSpec((LANES, tm), lambda i, j: (0, i))),
        scratch_shapes=[pltpu.VMEM((tm, d), BF16)],
        compiler_params=_cparams("arbitrary", "arbitrary"),
        name="in_proj",
    )(x, gain.reshape(1, d), w_main, w_dt)


def _t5_bucket_np(rel):
    n = np.maximum(rel, 0)
    max_exact = REL_BUCKETS // 2
    nf = np.maximum(n, 1).astype(np.float64)
    large = max_exact + (np.log(nf / max_exact) / math.log(REL_MAX_DIST / max_exact)
                         * (REL_BUCKETS - max_exact)).astype(np.int32)
    large = np.minimum(large, REL_BUCKETS - 1)
    return np.where(n < max_exact, n, large).astype(np.int32)


def _bucket_tiles(seq):
    qi = np.arange(MOBA_BLOCK)[:, None]
    ki = np.arange(MOBA_BLOCK)[None, :]
    own = _t5_bucket_np(qi - ki)
    prev = _t5_bucket_np(MOBA_BLOCK + qi - ki)
    far = _t5_bucket_np(np.arange(MOBA_BLOCK + 1, max(seq, MOBA_BLOCK + 2)))
    assert (far == far[0]).all(), "bias must be constant beyond the previous block"
    return np.stack([own, prev]), int(far[0])


def _bias_tiles_kernel(rb_ref, bk_ref, o_ref):
    h = pl.program_id(0)
    bk = bk_ref[...]
    acc = jnp.zeros(bk.shape, F32)
    for b in range(REL_BUCKETS):
        acc = jnp.where(bk == b, rb_ref[b, h], acc)
    o_ref[0] = acc


def _bias_tiles(rel_bias, buckets):
    nh = rel_bias.shape[1]
    return pl.pallas_call(
        _bias_tiles_kernel,
        out_shape=jax.ShapeDtypeStruct((nh, 2, MOBA_BLOCK, MOBA_BLOCK), F32),
        grid=(nh,),
        in_specs=[pl.BlockSpec(memory_space=pltpu.SMEM),
                  pl.BlockSpec((2, MOBA_BLOCK, MOBA_BLOCK), lambda h: (0, 0, 0))],
        out_specs=pl.BlockSpec((1, 2, MOBA_BLOCK, MOBA_BLOCK), lambda h: (h, 0, 0, 0)),
        compiler_params=_cparams("arbitrary"),
        name="bias_tiles",
    )(rel_bias, buckets)


def _moba_kernel(far_bucket, rb_ref, q_ref, k_ref, v_ref, bias_ref, o_ref, kmean_ref, kb_ref, vb_ref):
    hg = pl.program_id(1)
    c = pl.program_id(2)
    nh = kmean_ref.shape[0]
    nb = kmean_ref.shape[1]
    blk = MOBA_BLOCK
    hd = ATT_HEAD_DIM
    scale = hd ** -0.5

    @pl.when(c == 0)
    def _():
        k = k_ref[...]
        kb_ref[...] = k.astype(BF16)
        vb_ref[...] = v_ref[...].astype(BF16)
        for hh in range(nh):
            kk = k[:, hh * hd:(hh + 1) * hd].reshape(nb, blk, hd)
            kmean_ref[hh] = jnp.sum(kk, axis=1) * (1.0 / blk)

    lane = lax.broadcasted_iota(jnp.int32, (blk, nb), 1)
    past = lane < c
    row = lax.broadcasted_iota(jnp.int32, (blk, blk), 0)
    colk = lax.broadcasted_iota(jnp.int32, (blk, blk), 1)
    own_off = pl.multiple_of(c * blk, blk)

    qbs, sels, fars, init = [], [], [], []
    for hh in range(nh):
        hs = slice(hh * hd, (hh + 1) * hd)
        q = q_ref[:, hs]
        gate = _dot_nt(q, kmean_ref[hh], precision=HIGHEST)
        gm = jnp.where(past, gate, -jnp.inf)
        rank = jnp.zeros(gate.shape, jnp.int32)
        for j in range(nb):
            col = gm[:, j:j + 1]
            beats = (col > gm) | ((col == gm) & (j < lane))
            rank = rank + beats.astype(jnp.int32)
        sels.append(jnp.where(past & (rank < MOBA_TOPK), 1.0, 0.0))
        qb = q.astype(BF16)
        qbs.append(qb)
        fars.append(rb_ref[far_bucket, hg * nh + hh])
        s = _dot_nt(qb, kb_ref[pl.ds(own_off, blk), hs]) * scale + bias_ref[hh, 0]
        s = jnp.where(row >= colk, s, NEG)
        m0 = jnp.max(s, axis=1, keepdims=True)
        p = jnp.exp(s - m0)
        l0 = jnp.sum(p, axis=1, keepdims=True)
        acc0 = jnp.dot(p.astype(BF16), vb_ref[pl.ds(own_off, blk), hs], preferred_element_type=F32)
        init.append((m0, l0, acc0))

    def body(jp, carry):
        off = pl.multiple_of(jp * (2 * blk), 2 * blk)
        out = []
        for hh in range(nh):
            hs = slice(hh * hd, (hh + 1) * hd)
            m, l, acc = carry[hh]
            sj = _dot_nt(qbs[hh], kb_ref[pl.ds(off, 2 * blk), hs]) * scale
            halves = []
            for u in range(2):
                j = 2 * jp + u
                bias = jnp.where(j == c - 1, bias_ref[hh, 1], fars[hh])
                selcol = jnp.sum(jnp.where(lane == j, sels[hh], 0.0), axis=1, keepdims=True)
                halves.append(jnp.where(selcol > 0.0, sj[:, u * blk:(u + 1) * blk] + bias, NEG))
            sj = jnp.concatenate(halves, axis=1)
            m_new = jnp.maximum(m, jnp.max(sj, axis=1, keepdims=True))
            alpha = jnp.exp(m - m_new)
            pj = jnp.exp(sj - m_new)
            l_new = alpha * l + jnp.sum(pj, axis=1, keepdims=True)
            pv = jnp.dot(pj.astype(BF16), vb_ref[pl.ds(off, 2 * blk), hs], preferred_element_type=F32)
            out.append((m_new, l_new, alpha * acc + pv))
        return tuple(out)

    final = lax.fori_loop(0, (c + 1) // 2, body, tuple(init))
    for hh in range(nh):
        _, l, acc = final[hh]
        o_ref[:, hh * hd:(hh + 1) * hd] = acc / l


def _moba(proj, rel_bias, bias_tiles, far_bucket, bsz, seq):
    nb = seq // MOBA_BLOCK
    t = bsz * seq
    nh = MOBA_HEADS_PER_STEP
    gw = nh * ATT_HEAD_DIM
    ng = ATT_HEADS // nh
    return pl.pallas_call(
        functools.partial(_moba_kernel, far_bucket),
        out_shape=jax.ShapeDtypeStruct((t, ATT_WIDTH), F32),
        grid=(bsz, ng, nb),
        in_specs=[pl.BlockSpec(memory_space=pltpu.SMEM),
                  pl.BlockSpec((MOBA_BLOCK, gw), lambda b, g, c: (b * nb + c, g)),
                  pl.BlockSpec((seq, gw), lambda b, g, c: (b, ng + g)),
                  pl.BlockSpec((seq, gw), lambda b, g, c: (b, 2 * ng + g)),
                  pl.BlockSpec((nh, 2, MOBA_BLOCK, MOBA_BLOCK), lambda b, g, c: (g, 0, 0, 0))],
        out_specs=pl.BlockSpec((MOBA_BLOCK, gw), lambda b, g, c: (b * nb + c, g)),
        scratch_shapes=[pltpu.VMEM((nh, nb, ATT_HEAD_DIM), F32),
                        pltpu.VMEM((seq, gw), BF16),
                        pltpu.VMEM((seq, gw), BF16)],
        compiler_params=_cparams("arbitrary", "arbitrary", "arbitrary"),
        name="moba",
    )(rel_bias, proj, proj, proj, bias_tiles)


def _silu(x):
    return x / (1.0 + jnp.exp(-x))


def _softplus(x):
    return jnp.maximum(x, 0.0) + jnp.log(1.0 + jnp.exp(-jnp.abs(x)))


def _ssd_kernel(z_ref, xs_ref, bc_ref, dt_ref, dtt_ref, cwx_ref, cwbc_ref, cbx_ref, cbbc_ref,
                dtb_ref, dtbt_ref, a_ref, at_ref, dsk_ref, gain_ref, o_ref,
                extx_ref, extbc_ref, st_ref, y_ref):
    c = pl.program_id(1)
    L = SSD_CHUNK
    P = SSM_HEAD_DIM
    N = SSM_STATE
    hpg = SSM_HEADS // SSM_GROUPS

    @pl.when(c == 0)
    def _():
        extx_ref[0:SUBLANES, :] = jnp.zeros((SUBLANES, SSM_WIDTH), F32)
        extbc_ref[0:SUBLANES, :] = jnp.zeros((SUBLANES, 2 * SSM_GROUPS * N), F32)
        st_ref[...] = jnp.zeros(st_ref.shape, F32)

    extx_ref[SUBLANES:SUBLANES + L, :] = xs_ref[...]
    extbc_ref[SUBLANES:SUBLANES + L, :] = bc_ref[...]

    def conv(ext_ref, w_ref, b_ref):
        acc = b_ref[...]
        for i in range(SSM_CONV):
            acc = acc + ext_ref[SUBLANES - i:SUBLANES - i + L, :] * w_ref[SSM_CONV - 1 - i:SSM_CONV - i, :]
        return _silu(acc)

    xs = conv(extx_ref, cwx_ref, cbx_ref)
    bcm = conv(extbc_ref, cwbc_ref, cbbc_ref)
    extx_ref[0:SUBLANES, :] = extx_ref[L:L + SUBLANES, :]
    extbc_ref[0:SUBLANES, :] = extbc_ref[L:L + SUBLANES, :]

    dt = _softplus(dt_ref[...] + dtb_ref[...])
    dtt = _softplus(dtt_ref[...] + dtbt_ref[...])
    da = dt * (-jnp.exp(a_ref[...]))
    dat = dtt * (-jnp.exp(at_ref[...]))
    r = lax.broadcasted_iota(jnp.int32, (L, L), 0)
    s_ = lax.broadcasted_iota(jnp.int32, (L, L), 1)
    causal = r >= s_
    tril = jnp.where(causal, 1.0, 0.0)
    triu = jnp.where(r <= s_, 1.0, 0.0)
    cs = jnp.dot(tril, da, precision=HIGHEST, preferred_element_type=F32)
    cst = jnp.dot(dat, triu, precision=HIGHEST, preferred_element_type=F32)
    ecs = jnp.exp(cs)
    cs_last = cs[L - 1:L, :]
    dst = jnp.exp(cs_last - cs)
    cdec = jnp.exp(cs_last)

    for g in range(SSM_GROUPS):
        bg = bcm[:, g * N:(g + 1) * N]
        cg = bcm[:, (SSM_GROUPS + g) * N:(SSM_GROUPS + g + 1) * N]
        bgb = bg.astype(BF16)
        cgb = cg.astype(BF16)
        cb = _dot_nt(cgb, bgb)
        bgt = jnp.transpose(bg).astype(BF16)
        for hh in range(hpg):
            h = g * hpg + hh
            xh = xs[:, h * P:(h + 1) * P]
            xdt = xh * dt[:, h:h + 1]
            seg = cs[:, h:h + 1] - cst[h:h + 1, :]
            decay = jnp.exp(jnp.where(causal, seg, NEG))
            y = jnp.dot((cb * decay).astype(BF16), xdt.astype(BF16), preferred_element_type=F32)
            st = st_ref[h]
            y = y + jnp.dot(cgb, st.astype(BF16), preferred_element_type=F32) * ecs[:, h:h + 1]
            y = y + xh * dsk_ref[:, h:h + 1]
            y_ref[:, h * P:(h + 1) * P] = y
            new = jnp.dot(bgt, (xdt * dst[:, h:h + 1]).astype(BF16), preferred_element_type=F32)
            st_ref[h] = st * cdec[:, h:h + 1] + new

    y = y_ref[...] * _silu(z_ref[...])
    gw = SSM_WIDTH // SSM_GROUPS
    for g in range(SSM_GROUPS):
        yg = y[:, g * gw:(g + 1) * gw]
        o_ref[:, g * gw:(g + 1) * gw] = _rms(yg, gain_ref[:, g * gw:(g + 1) * gw]).astype(o_ref.dtype)


def _pad_lanes(v):
    return jnp.zeros((1, LANES), F32).at[0, :v.shape[0]].set(v.astype(F32))


def _ssd(proj, dt, dtt, conv_w, conv_b, dt_bias, a_log, d_skip, gain, bsz, seq):
    t = bsz * seq
    nc = seq // SSD_CHUNK
    L = SSD_CHUNK
    nbc = 2 * SSM_GROUPS * SSM_STATE
    zcol = 3 * ATT_WIDTH // SSM_WIDTH
    xcol = zcol + 1
    bccol = (3 * ATT_WIDTH + 2 * SSM_WIDTH) // nbc
    dtb = _pad_lanes(dt_bias)
    alog = jnp.full((1, LANES), -jnp.inf, F32).at[0, :SSM_HEADS].set(a_log.astype(F32))
    dsk = _pad_lanes(d_skip)
    full = lambda shape: pl.BlockSpec(shape, lambda b, c: tuple(0 for _ in shape))
    return pl.pallas_call(
        _ssd_kernel,
        out_shape=jax.ShapeDtypeStruct((t, SSM_WIDTH), BF16),
        grid=(bsz, nc),
        in_specs=[pl.BlockSpec((L, SSM_WIDTH), lambda b, c: (b * nc + c, zcol)),
                  pl.BlockSpec((L, SSM_WIDTH), lambda b, c: (b * nc + c, xcol)),
                  pl.BlockSpec((L, nbc), lambda b, c: (b * nc + c, bccol)),
                  pl.BlockSpec((L, LANES), lambda b, c: (b * nc + c, 0)),
                  pl.BlockSpec((LANES, L), lambda b, c: (0, b * nc + c)),
                  full((SSM_CONV, SSM_WIDTH)), full((SSM_CONV, nbc)),
                  full((1, SSM_WIDTH)), full((1, nbc)),
                  full((1, LANES)), full((LANES, 1)), full((1, LANES)), full((LANES, 1)),
                  full((1, LANES)), full((1, SSM_WIDTH))],
        out_specs=pl.BlockSpec((L, SSM_WIDTH), lambda b, c: (b * nc + c, 0)),
        scratch_shapes=[pltpu.VMEM((L + 2 * SUBLANES, SSM_WIDTH), F32),
                        pltpu.VMEM((L + 2 * SUBLANES, nbc), F32),
                        pltpu.VMEM((SSM_HEADS, SSM_STATE, SSM_HEAD_DIM), F32),
                        pltpu.VMEM((L, SSM_WIDTH), F32)],
        compiler_params=_cparams("arbitrary", "arbitrary"),
        name="ssd",
    )(proj, proj, proj, dt, dtt,
      conv_w[:, :SSM_WIDTH], conv_w[:, SSM_WIDTH:],
      conv_b[:SSM_WIDTH].reshape(1, -1), conv_b[SSM_WIDTH:].reshape(1, -1),
      dtb, dtb.reshape(LANES, 1), alog, alog.reshape(LANES, 1), dsk, gain.reshape(1, -1))


def _out_proj_kernel(att_ref, y_ref, g_ref, x_ref, w_ref, o_ref, mix_ref):
    @pl.when(pl.program_id(1) == 0)
    def _():
        mix_ref[:, :ATT_WIDTH] = _rms(att_ref[...], g_ref[...]).astype(BF16)
        mix_ref[:, ATT_WIDTH:] = y_ref[...]

    o_ref[...] = x_ref[...] + jnp.dot(mix_ref[...], w_ref[...], preferred_element_type=F32)


def _out_proj(att, y, gain, x, w, tm, tn):
    t, d = x.shape
    k = w.shape[0]
    return pl.pallas_call(
        _out_proj_kernel,
        out_shape=jax.ShapeDtypeStruct((t, d), F32),
        grid=(t // tm, d // tn),
        in_specs=[pl.BlockSpec((tm, ATT_WIDTH), lambda i, j: (i, 0)),
                  pl.BlockSpec((tm, SSM_WIDTH), lambda i, j: (i, 0)),
                  pl.BlockSpec((1, ATT_WIDTH), lambda i, j: (0, 0)),
                  pl.BlockSpec((tm, tn), lambda i, j: (i, j)),
                  pl.BlockSpec((k, tn), lambda i, j: (0, j))],
        out_specs=pl.BlockSpec((tm, tn), lambda i, j: (i, j)),
        scratch_shapes=[pltpu.VMEM((tm, k), BF16)],
        compiler_params=_cparams("arbitrary", "arbitrary"),
        name="out_proj",
    )(att, y, gain.reshape(1, -1), x, w)


def _peer_topk_kernel(q_ref, sk_ref, e_ref, g_ref, v_ref, i_ref, top_ref):
    tb = q_ref.shape[0]
    K = PEER_TOPK
    S = SUBLANES
    key_iota = lax.broadcasted_iota(jnp.int32, (PEER_NKEYS, tb), 0)
    sub = lax.broadcasted_iota(jnp.int32, (S, tb), 0)
    for h in range(PEER_HEADS):
        for c in range(2):
            col = (2 * h + c) * PEER_HALF
            sc = _dot_nt(sk_ref[h, c], q_ref[:, col:col + PEER_HALF], precision=HIGHEST)
            for i in range(K):
                m = jnp.max(sc, axis=0, keepdims=True)
                idx = jnp.min(jnp.where(sc == m, key_iota, PEER_NKEYS), axis=0, keepdims=True)
                v_ref[c, i:i + 1, :] = m
                i_ref[c, i:i + 1, :] = idx
                sc = jnp.where(key_iota == idx, -jnp.inf, sc)
        v2lo, v2hi = v_ref[1, 0:S, :], v_ref[1, S:K, :]
        i2lo, i2hi = i_ref[1, 0:S, :], i_ref[1, S:K, :]
        cands, eids, flats = [], [], []
        for a in range(S):
            cands.append(v_ref[0, a:a + 1, :] + v2lo)
            eids.append(i_ref[0, a:a + 1, :] * PEER_NKEYS + i2lo)
            flats.append(a * K + sub)
        cands.append(v_ref[0, 0:1, :] + v2hi)
        eids.append(i_ref[0, 0:1, :] * PEER_NKEYS + i2hi)
        flats.append(S + sub)
        cands.append(v_ref[0, S:K, :] + v_ref[1, 0:1, :])
        eids.append(i_ref[0, S:K, :] * PEER_NKEYS + i_ref[1, 0:1, :])
        flats.append((S + sub) * K)
        for i in range(K):
            m = functools.reduce(jnp.maximum, cands)
            m = jnp.max(m, axis=0, keepdims=True)
            pos = functools.reduce(jnp.minimum, [jnp.where(cd == m, fl, K * K) for cd, fl in zip(cands, flats)])
            pos = jnp.min(pos, axis=0, keepdims=True)
            hits = [fl == pos for fl in flats]
            e = functools.reduce(jnp.add, [jnp.where(ht, ei, 0) for ht, ei in zip(hits, eids)])
            e_ref[h * K + i:h * K + i + 1, :] = jnp.sum(e, axis=0, keepdims=True)
            top_ref[i:i + 1, :] = m
            cands = [jnp.where(ht, -jnp.inf, cd) for ht, cd in zip(hits, cands)]
        top_s = top_ref[...]
        p = jnp.exp(top_s - top_s[0:1, :])
        g_ref[h * K:(h + 1) * K, :] = p / jnp.sum(p, axis=0, keepdims=True)


def _peer_topk(q, subkeys, tb):
    t, d = q.shape
    return pl.pallas_call(
        _peer_topk_kernel,
        out_shape=(jax.ShapeDtypeStruct((PEER_HK, t), jnp.int32),
                   jax.ShapeDtypeStruct((PEER_HK, t), F32)),
        grid=(t // tb,),
        in_specs=[pl.BlockSpec((tb, d), lambda i: (i, 0)),
                  pl.BlockSpec(subkeys.shape, lambda i: (0, 0, 0, 0))],
        out_specs=(pl.BlockSpec((PEER_HK, tb), lambda i: (0, i)),
                   pl.BlockSpec((PEER_HK, tb), lambda i: (0, i))),
        scratch_shapes=[pltpu.VMEM((2, PEER_TOPK, tb), F32),
                        pltpu.VMEM((2, PEER_TOPK, tb), jnp.int32),
                        pltpu.VMEM((PEER_TOPK, tb), F32)],
        compiler_params=_cparams("arbitrary"),
        name="peer_topk",
    )(q, subkeys)


def _gelu(x):
    return 0.5 * x * (1.0 + lax.erf(x * (1.0 / math.sqrt(2.0))))


def _unpack_bf16_pair(w):
    lo = lax.bitcast_convert_type(w << 16, F32)
    hi = lax.bitcast_convert_type(w & jnp.uint32(0xFFFF0000), F32)
    return lo, hi


def _peer_experts_kernel(e0_ref, e1_ref, e2_ref, x_ref, gates_ref, fg_ref, og_ref, uv_hbm,
                         o_ref, buf0, buf1, buf2, sem):
    i = pl.program_id(0)
    n = pl.num_programs(0)
    tpb = x_ref.shape[0]
    nw = x_ref.shape[1] // (2 * LANES)
    rows = tpb * PEER_HK
    pitch = PEER_ROW_PITCH
    bufs = (buf0, buf1, buf2)
    nbuf = len(bufs)

    def issue_token(e_ref, j, dst, dst_sem):
        for k in range(PEER_HK):
            r = j * PEER_HK + k
            e = e_ref[r]
            src = uv_hbm.at[pl.ds(pl.multiple_of(e * (2 * nw), 2 * nw), 2 * nw), :]
            pltpu.make_async_copy(src, dst.at[pl.ds(r * pitch, 2 * nw), :], dst_sem).start(priority=k % 2)

    def wait_all(b):
        window = pl.ds(0, rows * 2 * nw)
        pltpu.make_async_copy(bufs[(b + 1) % nbuf].at[window, :], bufs[b].at[window, :], sem.at[b]).wait()

    def chunk(buf, j, s):
        return buf[pl.ds(j * PEER_HK * pitch + s, PEER_HK, stride=pitch), :]

    @pl.when(i == 0)
    def _():
        for j in range(tpb):
            issue_token(e0_ref, j, buf0, sem.at[0])
            issue_token(e1_ref, j, buf1, sem.at[1])

    def step(b):
        ahead = (b + 2) % nbuf
        cur, nxt = bufs[b], bufs[ahead]
        wait_all(b)
        x = x_ref[...]
        hx = _rms(x, fg_ref[...])
        gates = gates_ref[...]
        eye = (lax.broadcasted_iota(jnp.int32, (PEER_HK, PEER_HK), 0)
               == lax.broadcasted_iota(jnp.int32, (PEER_HK, PEER_HK), 1))
        outs = []
        for j in range(tpb):
            acc = jnp.zeros((PEER_HK, LANES), F32)
            for s in range(nw):
                lo, hi = _unpack_bf16_pair(chunk(cur, j, s))
                acc = acc + lo * hx[j:j + 1, s * LANES:(s + 1) * LANES]
                acc = acc + hi * hx[j:j + 1, (nw + s) * LANES:(nw + s + 1) * LANES]
            pre = jnp.sum(acc, axis=1, keepdims=True)
            gcol = jnp.sum(jnp.where(eye, gates[j:j + 1, :], 0.0), axis=1, keepdims=True)
            act = _gelu(pre) * gcol
            los, his = [], []
            for s in range(nw):
                lo, hi = _unpack_bf16_pair(chunk(cur, j, nw + s))
                los.append(jnp.sum(lo * act, axis=0, keepdims=True))
                his.append(jnp.sum(hi * act, axis=0, keepdims=True))
            outs.append(jnp.concatenate(los + his, axis=1))
            issue_token(e2_ref, j, nxt, sem.at[ahead])
        x2 = x + jnp.concatenate(outs, axis=0)
        o_ref[...] = _rms(x2, og_ref[...])

        @pl.when(i == n - 1)
        def _():
            wait_all((b + 1) % nbuf)
            wait_all(ahead)

    for b in range(nbuf):
        pl.when(lax.rem(i, nbuf) == b)(functools.partial(step, b))


def _peer_experts(x1, eidx, gates, ffn_gain, out_gain, uv_tab):
    t, d = x1.shape
    tpb = PEER_TOKENS_PER_STEP
    n = t // tpb
    rows = tpb * PEER_HK
    nw = d // (2 * LANES)
    ids = eidx.reshape(t * PEER_HK)
    smem_blk = lambda ahead: pl.BlockSpec((rows,), lambda i: (jnp.minimum(i + ahead, n - 1),),
                                          memory_space=pltpu.SMEM)
    buf = pltpu.VMEM((rows * PEER_ROW_PITCH, LANES), jnp.uint32)
    return pl.pallas_call(
        _peer_experts_kernel,
        out_shape=jax.ShapeDtypeStruct((t, d), F32),
        grid=(n,),
        in_specs=[smem_blk(0), smem_blk(1), smem_blk(2),
                  pl.BlockSpec((tpb, d), lambda i: (i, 0)),
                  pl.BlockSpec((tpb, PEER_HK), lambda i: (i, 0)),
                  pl.BlockSpec((1, d), lambda i: (0, 0)),
                  pl.BlockSpec((1, d), lambda i: (0, 0)),
                  pl.BlockSpec(memory_space=pl.ANY)],
        out_specs=pl.BlockSpec((tpb, d), lambda i: (i, 0)),
        scratch_shapes=[buf, buf, buf, pltpu.SemaphoreType.DMA((3,))],
        compiler_params=_cparams("arbitrary"),
        name="peer_experts",
    )(ids, ids, ids, x1, gates, ffn_gain.reshape(1, d), out_gain.reshape(1, d), uv_tab)


def _pack_tables_kernel(u_ref, v_ref, o_ref):
    eb, d = u_ref.shape
    nw = d // (2 * LANES)

    def bf16_bits(x):
        return lax.bitcast_convert_type(x.astype(BF16).astype(F32), jnp.uint32)

    for t, ref in enumerate((u_ref, v_ref)):
        for s in range(nw):
            lo = bf16_bits(ref[:, s * LANES:(s + 1) * LANES]) >> 16
            hi = bf16_bits(ref[:, (nw + s) * LANES:(nw + s + 1) * LANES])
            o_ref[pl.ds(t * nw + s, eb, stride=2 * nw), :] = hi | lo


def _merge_expert_tables(u_tab, v_tab):
    ne, d = u_tab.shape
    nw = d // (2 * LANES)
    eb = 256
    return pl.pallas_call(
        _pack_tables_kernel,
        out_shape=jax.ShapeDtypeStruct((ne * 2 * nw, LANES), jnp.uint32),
        grid=(ne // eb,),
        in_specs=[pl.BlockSpec((eb, d), lambda i: (i, 0)),
                  pl.BlockSpec((eb, d), lambda i: (i, 0))],
        out_specs=pl.BlockSpec((eb * 2 * nw, LANES), lambda i: (i, 0)),
        compiler_params=_cparams("arbitrary"),
        name="pack_tables",
    )(u_tab, v_tab)


def _layer(x2d, bsz, seq, attn_norm, w_in, conv_w, conv_b, dt_bias, a_log, d_skip, rel_bias,
           attn_out_norm, ssm_out_norm, w_out, ffn_norm, peer_wq, peer_subkeys, peer_u, peer_v,
           out_gain, buckets, far_bucket):
    n_main = 3 * ATT_WIDTH + 2 * SSM_WIDTH + 2 * SSM_GROUPS * SSM_STATE
    w_main = w_in[:, :n_main].astype(BF16)
    w_dt = jnp.zeros((w_in.shape[0], LANES), F32).at[:, :SSM_HEADS].set(w_in[:, n_main:])
    proj, dt, dtt = _in_proj(x2d, attn_norm, w_main, w_dt, 512, 512)
    tiles = _bias_tiles(rel_bias, buckets)
    att = _moba(proj, rel_bias, tiles, far_bucket, bsz, seq)
    y = _ssd(proj, dt, dtt, conv_w, conv_b, dt_bias, a_log, d_skip, ssm_out_norm, bsz, seq)
    x1 = _out_proj(att, y, attn_out_norm, x2d, w_out.astype(BF16), 512, 1024)
    pq = _norm_matmul(x1, ffn_norm, peer_wq.astype(BF16), 512, 1024)
    e_t, g_t = _peer_topk(pq, peer_subkeys, LANES)
    uv = _merge_expert_tables(peer_u, peer_v)
    return _peer_experts(x1, e_t.T, g_t.T, ffn_norm, out_gain, uv)


def kernel(x, attn_norm, w_in, conv_w, conv_b, dt_bias, a_log, d_skip, rel_bias, attn_out_norm,
           ssm_out_norm, w_out, ffn_norm, peer_wq, peer_subkeys, peer_u, peer_v, final_norm):
    bsz, seq, d = x.shape
    depth = w_in.shape[0]
    assert depth == 1, "the final rmsnorm is fused into the last layer's expert kernel"
    buckets_np, far_bucket = _bucket_tiles(seq)
    buckets = jnp.asarray(buckets_np)
    out = _layer(x.reshape(bsz * seq, d), bsz, seq, attn_norm[0], w_in[0], conv_w[0], conv_b[0],
                 dt_bias[0], a_log[0], d_skip[0], rel_bias, attn_out_norm[0], ssm_out_norm[0],
                 w_out[0], ffn_norm[0], peer_wq[0], peer_subkeys[0], peer_u[0], peer_v[0],
                 final_norm, buckets, far_bucket)
    return out.reshape(bsz, seq, d)
```

```python
import functools
import math

import numpy as np
import jax
import jax.numpy as jnp
from jax import lax
from jax.experimental import pallas as pl
from jax.experimental.pallas import tpu as pltpu

F32 = jnp.float32
BF16 = jnp.bfloat16
EPS = 1e-6
NEG = -1e30

LANES = 128
SUBLANES = 8
VMEM_LIMIT = 56 * 1024 * 1024

ATT_HEADS = 8
ATT_HEAD_DIM = 128
ATT_WIDTH = ATT_HEADS * ATT_HEAD_DIM
MOBA_BLOCK = 256
MOBA_TOPK = 3
MOBA_HEADS_PER_STEP = 2
REL_BUCKETS = 32
REL_MAX_DIST = 128
SSM_HEADS = 16
SSM_HEAD_DIM = 64
SSM_WIDTH = SSM_HEADS * SSM_HEAD_DIM
SSM_GROUPS = 2
SSM_STATE = 128
SSM_CONV = 4
SSD_CHUNK = 128
PEER_HEADS = 8
PEER_NKEYS = 128
PEER_HALF = 128
PEER_TOPK = 16
PEER_HK = PEER_HEADS * PEER_TOPK
PEER_TOKENS_PER_STEP = 8
PEER_ROW_PITCH = 20

HIGHEST = lax.Precision.HIGHEST


def _cparams(*sem):
    return pltpu.CompilerParams(dimension_semantics=sem, vmem_limit_bytes=VMEM_LIMIT)


def _rms(x, gain):
    return x * lax.rsqrt(jnp.mean(x * x, axis=-1, keepdims=True) + EPS) * gain


def _dot_nt(a, b, precision=None):
    return lax.dot_general(a, b, (((1,), (1,)), ((), ())), precision=precision,
                           preferred_element_type=F32)


def _norm_matmul_kernel(x_ref, g_ref, w_ref, o_ref, hn_ref):
    @pl.when(pl.program_id(1) == 0)
    def _():
        hn_ref[...] = _rms(x_ref[...], g_ref[...]).astype(BF16)

    o_ref[...] = jnp.dot(hn_ref[...], w_ref[...], preferred_element_type=F32)


def _norm_matmul(x, gain, w, tm, tn):
    t, d = x.shape
    n = w.shape[1]
    return pl.pallas_call(
        _norm_matmul_kernel,
        out_shape=jax.ShapeDtypeStruct((t, n), F32),
        grid=(t // tm, n // tn),
        in_specs=[pl.BlockSpec((tm, d), lambda i, j: (i, 0)),
                  pl.BlockSpec((1, d), lambda i, j: (0, 0)),
                  pl.BlockSpec((d, tn), lambda i, j: (0, j))],
        out_specs=pl.BlockSpec((tm, tn), lambda i, j: (i, j)),
        scratch_shapes=[pltpu.VMEM((tm, d), BF16)],
        compiler_params=_cparams("arbitrary", "arbitrary"),
        name="norm_matmul",
    )(x, gain.reshape(1, d), w)


def _in_proj_kernel(x_ref, g_ref, w_ref, wdt_ref, o_ref, dt_ref, dtt_ref, hn_ref):
    @pl.when(pl.program_id(1) == 0)
    def _():
        hn = _rms(x_ref[...], g_ref[...])
        hn_ref[...] = hn.astype(BF16)
        dt = jnp.dot(hn, wdt_ref[...], precision=HIGHEST, preferred_element_type=F32)
        dt_ref[...] = dt
        dtt_ref[...] = dt.T

    o_ref[...] = jnp.dot(hn_ref[...], w_ref[...], preferred_element_type=F32)


def _in_proj(x, gain, w_main, w_dt, tm, tn):
    t, d = x.shape
    n = w_main.shape[1]
    return pl.pallas_call(
        _in_proj_kernel,
        out_shape=(jax.ShapeDtypeStruct((t, n), F32),
                   jax.ShapeDtypeStruct((t, LANES), F32),
                   jax.ShapeDtypeStruct((LANES, t), F32)),
        grid=(t // tm, n // tn),
        in_specs=[pl.BlockSpec((tm, d), lambda i, j: (i, 0)),
                  pl.BlockSpec((1, d), lambda i, j: (0, 0)),
                  pl.BlockSpec((d, tn), lambda i, j: (0, j)),
                  pl.BlockSpec((d, LANES), lambda i, j: (0, 0))],
        out_specs=(pl.BlockSpec((tm, tn), lambda i, j: (i, j)),
                   pl.BlockSpec((tm, LANES), lambda i, j: (i, 0)),
                   pl.BlockSpec((LANES, tm), lambda i, j: (0, i))),
        scratch_shapes=[pltpu.VMEM((tm, d), BF16)],
        compiler_params=_cparams("arbitrary", "arbitrary"),
        name="in_proj",
    )(x, gain.reshape(1, d), w_main, w_dt)


def _t5_bucket_np(rel):
    n = np.maximum(rel, 0)
    max_exact = REL_BUCKETS // 2
    nf = np.maximum(n, 1).astype(np.float64)
    large = max_exact + (np.log(nf / max_exact) / math.log(REL_MAX_DIST / max_exact)
                         * (REL_BUCKETS - max_exact)).astype(np.int32)
    large = np.minimum(large, REL_BUCKETS - 1)
    return np.where(n < max_exact, n, large).astype(np.int32)


def _bucket_tiles(seq):
    qi = np.arange(MOBA_BLOCK)[:, None]
    ki = np.arange(MOBA_BLOCK)[None, :]
    own = _t5_bucket_np(qi - ki)
    prev = _t5_bucket_np(MOBA_BLOCK + qi - ki)
    far = _t5_bucket_np(np.arange(MOBA_BLOCK + 1, max(seq, MOBA_BLOCK + 2)))
    assert (far == far[0]).all(), "bias must be constant beyond the previous block"
    return np.stack([own, prev]), int(far[0])


def _bias_tiles_kernel(rb_ref, bk_ref, o_ref):
    h = pl.program_id(0)
    bk = bk_ref[...]
    acc = jnp.zeros(bk.shape, F32)
    for b in range(REL_BUCKETS):
        acc = jnp.where(bk == b, rb_ref[b, h], acc)
    o_ref[0] = acc


def _bias_tiles(rel_bias, buckets):
    nh = rel_bias.shape[1]
    return pl.pallas_call(
        _bias_tiles_kernel,
        out_shape=jax.ShapeDtypeStruct((nh, 2, MOBA_BLOCK, MOBA_BLOCK), F32),
        grid=(nh,),
        in_specs=[pl.BlockSpec(memory_space=pltpu.SMEM),
                  pl.BlockSpec((2, MOBA_BLOCK, MOBA_BLOCK), lambda h: (0, 0, 0))],
        out_specs=pl.BlockSpec((1, 2, MOBA_BLOCK, MOBA_BLOCK), lambda h: (h, 0, 0, 0)),
        compiler_params=_cparams("arbitrary"),
        name="bias_tiles",
    )(rel_bias, buckets)


def _moba_kernel(far_bucket, rb_ref, q_ref, k_ref, v_ref, bias_ref, o_ref, kmean_ref, kb_ref, vb_ref):
    hg = pl.program_id(1)
    c = pl.program_id(2)
    nh = kmean_ref.shape[0]
    nb = kmean_ref.shape[1]
    blk = MOBA_BLOCK
    hd = ATT_HEAD_DIM
    scale = hd ** -0.5

    @pl.when(c == 0)
    def _():
        k = k_ref[...]
        kb_ref[...] = k.astype(BF16)
        vb_ref[...] = v_ref[...].astype(BF16)
        for hh in range(nh):
            kk = k[:, hh * hd:(hh + 1) * hd].reshape(nb, blk, hd)
            kmean_ref[hh] = jnp.sum(kk, axis=1) * (1.0 / blk)

    blk_id = lax.broadcasted_iota(jnp.int32, (nb, blk), 0)
    past = blk_id < c
    row = lax.broadcasted_iota(jnp.int32, (blk, blk), 0)
    colk = lax.broadcasted_iota(jnp.int32, (blk, blk), 1)
    own_off = pl.multiple_of(c * blk, blk)
    pair_blk = lax.broadcasted_iota(jnp.int32, (nb, 2 * blk), 0)
    pair_half = jnp.where(lax.broadcasted_iota(jnp.int32, (nb, 2 * blk), 1) >= blk, 1, 0)

    qbs, sels, fars, init = [], [], [], []
    for hh in range(nh):
        hs = slice(hh * hd, (hh + 1) * hd)
        q = q_ref[:, hs]
        gate = _dot_nt(kmean_ref[hh], q, precision=HIGHEST)
        gm = jnp.where(past, gate, -jnp.inf)
        rank = jnp.zeros(gate.shape, jnp.int32)
        for j in range(nb):
            gj = gm[j:j + 1, :]
            beats = (gj > gm) | ((gj == gm) & (j < blk_id))
            rank = rank + beats.astype(jnp.int32)
        sels.append(jnp.where(past & (rank < MOBA_TOPK), 1.0, 0.0))
        qb = q.astype(BF16)
        qbs.append(qb)
        fars.append(rb_ref[far_bucket, hg * nh + hh])
        s = _dot_nt(qb, kb_ref[pl.ds(own_off, blk), hs]) * scale + bias_ref[hh, 0]
        s = jnp.where(row >= colk, s, NEG)
        m0 = jnp.max(s, axis=1, keepdims=True)
        p = jnp.exp(s - m0)
        l0 = jnp.sum(p, axis=1, keepdims=True)
        acc0 = jnp.dot(p.astype(BF16), vb_ref[pl.ds(own_off, blk), hs], preferred_element_type=F32)
        init.append((m0, l0, acc0))

    def body(jp, carry):
        off = pl.multiple_of(jp * (2 * blk), 2 * blk)
        pick = jnp.where(pair_blk == 2 * jp + pair_half, 1.0, 0.0)
        out = []
        for hh in range(nh):
            hs = slice(hh * hd, (hh + 1) * hd)
            m, l, acc = carry[hh]
            sj = _dot_nt(qbs[hh], kb_ref[pl.ds(off, 2 * blk), hs]) * scale
            chosen = lax.dot_general(sels[hh], pick, (((0,), (0,)), ((), ())),
                                     preferred_element_type=F32)
            bias = jnp.concatenate([jnp.where(2 * jp + u == c - 1, bias_ref[hh, 1], fars[hh])
                                    for u in range(2)], axis=1)
            sj = jnp.where(chosen > 0.0, sj + bias, NEG)
            m_new = jnp.maximum(m, jnp.max(sj, axis=1, keepdims=True))
            alpha = jnp.exp(m - m_new)
            pj = jnp.exp(sj - m_new)
            l_new = alpha * l + jnp.sum(pj, axis=1, keepdims=True)
            pv = jnp.dot(pj.astype(BF16), vb_ref[pl.ds(off, 2 * blk), hs], preferred_element_type=F32)
            out.append((m_new, l_new, alpha * acc + pv))
        return tuple(out)

    final = lax.fori_loop(0, (c + 1) // 2, body, tuple(init))
    for hh in range(nh):
        _, l, acc = final[hh]
        o_ref[:, hh * hd:(hh + 1) * hd] = acc / l


def _moba(proj, rel_bias, bias_tiles, far_bucket, bsz, seq):
    nb = seq // MOBA_BLOCK
    t = bsz * seq
    nh = MOBA_HEADS_PER_STEP
    gw = nh * ATT_HEAD_DIM
    ng = ATT_HEADS // nh
    return pl.pallas_call(
        functools.partial(_moba_kernel, far_bucket),
        out_shape=jax.ShapeDtypeStruct((t, ATT_WIDTH), F32),
        grid=(bsz, ng, nb),
        in_specs=[pl.BlockSpec(memory_space=pltpu.SMEM),
                  pl.BlockSpec((MOBA_BLOCK, gw), lambda b, g, c: (b * nb + c, g)),
                  pl.BlockSpec((seq, gw), lambda b, g, c: (b, ng + g)),
                  pl.BlockSpec((seq, gw), lambda b, g, c: (b, 2 * ng + g)),
                  pl.BlockSpec((nh, 2, MOBA_BLOCK, MOBA_BLOCK), lambda b, g, c: (g, 0, 0, 0))],
        out_specs=pl.BlockSpec((MOBA_BLOCK, gw), lambda b, g, c: (b * nb + c, g)),
        scratch_shapes=[pltpu.VMEM((nh, nb, ATT_HEAD_DIM), F32),
                        pltpu.VMEM((seq, gw), BF16),
                        pltpu.VMEM((seq, gw), BF16)],
        compiler_params=_cparams("arbitrary", "arbitrary", "arbitrary"),
        name="moba",
    )(rel_bias, proj, proj, proj, bias_tiles)


def _silu(x):
    return x / (1.0 + jnp.exp(-x))


def _softplus(x):
    return jnp.maximum(x, 0.0) + jnp.log(1.0 + jnp.exp(-jnp.abs(x)))


def _ssd_kernel(z_ref, xs_ref, bc_ref, dt_ref, dtt_ref, cwx_ref, cwbc_ref, cbx_ref, cbbc_ref,
                dtb_ref, dtbt_ref, a_ref, at_ref, dsk_ref, gain_ref, o_ref,
                extx_ref, extbc_ref, st_ref, y_ref):
    c = pl.program_id(1)
    L = SSD_CHUNK
    P = SSM_HEAD_DIM
    N = SSM_STATE
    hpg = SSM_HEADS // SSM_GROUPS

    @pl.when(c == 0)
    def _():
        extx_ref[0:SUBLANES, :] = jnp.zeros((SUBLANES, SSM_WIDTH), F32)
        extbc_ref[0:SUBLANES, :] = jnp.zeros((SUBLANES, 2 * SSM_GROUPS * N), F32)
        st_ref[...] = jnp.zeros(st_ref.shape, F32)

    extx_ref[SUBLANES:SUBLANES + L, :] = xs_ref[...]
    extbc_ref[SUBLANES:SUBLANES + L, :] = bc_ref[...]

    def conv(ext_ref, w_ref, b_ref):
        acc = b_ref[...]
        for i in range(SSM_CONV):
            acc = acc + ext_ref[SUBLANES - i:SUBLANES - i + L, :] * w_ref[SSM_CONV - 1 - i:SSM_CONV - i, :]
        return _silu(acc)

    xs = conv(extx_ref, cwx_ref, cbx_ref)
    bcm = conv(extbc_ref, cwbc_ref, cbbc_ref)
    extx_ref[0:SUBLANES, :] = extx_ref[L:L + SUBLANES, :]
    extbc_ref[0:SUBLANES, :] = extbc_ref[L:L + SUBLANES, :]

    dt = _softplus(dt_ref[...] + dtb_ref[...])
    dtt = _softplus(dtt_ref[...] + dtbt_ref[...])
    da = dt * (-jnp.exp(a_ref[...]))
    dat = dtt * (-jnp.exp(at_ref[...]))
    r = lax.broadcasted_iota(jnp.int32, (L, L), 0)
    s_ = lax.broadcasted_iota(jnp.int32, (L, L), 1)
    causal = r >= s_
    tril = jnp.where(causal, 1.0, 0.0)
    triu = jnp.where(r <= s_, 1.0, 0.0)
    cs = jnp.dot(tril, da, precision=HIGHEST, preferred_element_type=F32)
    cst = jnp.dot(dat, triu, precision=HIGHEST, preferred_element_type=F32)
    ecs = jnp.exp(cs)
    cs_last = cs[L - 1:L, :]
    dst = jnp.exp(cs_last - cs)
    cdec = jnp.exp(cs_last)

    for g in range(SSM_GROUPS):
        bg = bcm[:, g * N:(g + 1) * N]
        cg = bcm[:, (SSM_GROUPS + g) * N:(SSM_GROUPS + g + 1) * N]
        bgb = bg.astype(BF16)
        cgb = cg.astype(BF16)
        cb = _dot_nt(cgb, bgb)
        bgt = jnp.transpose(bg).astype(BF16)
        for hh in range(hpg):
            h = g * hpg + hh
            xh = xs[:, h * P:(h + 1) * P]
            xdt = xh * dt[:, h:h + 1]
            seg = cs[:, h:h + 1] - cst[h:h + 1, :]
            decay = jnp.exp(jnp.where(causal, seg, NEG))
            y = jnp.dot((cb * decay).astype(BF16), xdt.astype(BF16), preferred_element_type=F32)
            st = st_ref[h]
            y = y + jnp.dot(cgb, st.astype(BF16), preferred_element_type=F32) * ecs[:, h:h + 1]
            y = y + xh * dsk_ref[:, h:h + 1]
            y_ref[:, h * P:(h + 1) * P] = y
            new = jnp.dot(bgt, (xdt * dst[:, h:h + 1]).astype(BF16), preferred_element_type=F32)
            st_ref[h] = st * cdec[:, h:h + 1] + new

    y = y_ref[...] * _silu(z_ref[...])
    gw = SSM_WIDTH // SSM_GROUPS
    for g in range(SSM_GROUPS):
        yg = y[:, g * gw:(g + 1) * gw]
        o_ref[:, g * gw:(g + 1) * gw] = _rms(yg, gain_ref[:, g * gw:(g + 1) * gw]).astype(o_ref.dtype)


def _pad_lanes(v):
    return jnp.zeros((1, LANES), F32).at[0, :v.shape[0]].set(v.astype(F32))


def _ssd(proj, dt, dtt, conv_w, conv_b, dt_bias, a_log, d_skip, gain, bsz, seq):
    t = bsz * seq
    nc = seq // SSD_CHUNK
    L = SSD_CHUNK
    nbc = 2 * SSM_GROUPS * SSM_STATE
    zcol = 3 * ATT_WIDTH // SSM_WIDTH
    xcol = zcol + 1
    bccol = (3 * ATT_WIDTH + 2 * SSM_WIDTH) // nbc
    dtb = _pad_lanes(dt_bias)
    alog = jnp.full((1, LANES), -jnp.inf, F32).at[0, :SSM_HEADS].set(a_log.astype(F32))
    dsk = _pad_lanes(d_skip)
    full = lambda shape: pl.BlockSpec(shape, lambda b, c: tuple(0 for _ in shape))
    return pl.pallas_call(
        _ssd_kernel,
        out_shape=jax.ShapeDtypeStruct((t, SSM_WIDTH), BF16),
        grid=(bsz, nc),
        in_specs=[pl.BlockSpec((L, SSM_WIDTH), lambda b, c: (b * nc + c, zcol)),
                  pl.BlockSpec((L, SSM_WIDTH), lambda b, c: (b * nc + c, xcol)),
                  pl.BlockSpec((L, nbc), lambda b, c: (b * nc + c, bccol)),
                  pl.BlockSpec((L, LANES), lambda b, c: (b * nc + c, 0)),
                  pl.BlockSpec((LANES, L), lambda b, c: (0, b * nc + c)),
                  full((SSM_CONV, SSM_WIDTH)), full((SSM_CONV, nbc)),
                  full((1, SSM_WIDTH)), full((1, nbc)),
                  full((1, LANES)), full((LANES, 1)), full((1, LANES)), full((LANES, 1)),
                  full((1, LANES)), full((1, SSM_WIDTH))],
        out_specs=pl.BlockSpec((L, SSM_WIDTH), lambda b, c: (b * nc + c, 0)),
        scratch_shapes=[pltpu.VMEM((L + 2 * SUBLANES, SSM_WIDTH), F32),
                        pltpu.VMEM((L + 2 * SUBLANES, nbc), F32),
                        pltpu.VMEM((SSM_HEADS, SSM_STATE, SSM_HEAD_DIM), F32),
                        pltpu.VMEM((L, SSM_WIDTH), F32)],
        compiler_params=_cparams("arbitrary", "arbitrary"),
        name="ssd",
    )(proj, proj, proj, dt, dtt,
      conv_w[:, :SSM_WIDTH], conv_w[:, SSM_WIDTH:],
      conv_b[:SSM_WIDTH].reshape(1, -1), conv_b[SSM_WIDTH:].reshape(1, -1),
      dtb, dtb.reshape(LANES, 1), alog, alog.reshape(LANES, 1), dsk, gain.reshape(1, -1))


def _out_proj_kernel(att_ref, y_ref, g_ref, x_ref, w_ref, o_ref, mix_ref):
    @pl.when(pl.program_id(1) == 0)
    def _():
        mix_ref[:, :ATT_WIDTH] = _rms(att_ref[...], g_ref[...]).astype(BF16)
        mix_ref[:, ATT_WIDTH:] = y_ref[...]

    o_ref[...] = x_ref[...] + jnp.dot(mix_ref[...], w_ref[...], preferred_element_type=F32)


def _out_proj(att, y, gain, x, w, tm, tn):
    t, d = x.shape
    k = w.shape[0]
    return pl.pallas_call(
        _out_proj_kernel,
        out_shape=jax.ShapeDtypeStruct((t, d), F32),
        grid=(t // tm, d // tn),
        in_specs=[pl.BlockSpec((tm, ATT_WIDTH), lambda i, j: (i, 0)),
                  pl.BlockSpec((tm, SSM_WIDTH), lambda i, j: (i, 0)),
                  pl.BlockSpec((1, ATT_WIDTH), lambda i, j: (0, 0)),
                  pl.BlockSpec((tm, tn), lambda i, j: (i, j)),
                  pl.BlockSpec((k, tn), lambda i, j: (0, j))],
        out_specs=pl.BlockSpec((tm, tn), lambda i, j: (i, j)),
        scratch_shapes=[pltpu.VMEM((tm, k), BF16)],
        compiler_params=_cparams("arbitrary", "arbitrary"),
        name="out_proj",
    )(att, y, gain.reshape(1, -1), x, w)


def _peer_topk_kernel(q_ref, sk_ref, e_ref, g_ref, v_ref, i_ref, top_ref, etop_ref):
    tb = q_ref.shape[0]
    K = PEER_TOPK
    S = SUBLANES
    key_iota = lax.broadcasted_iota(jnp.int32, (PEER_NKEYS, tb), 0).astype(F32)
    sub = lax.broadcasted_iota(jnp.int32, (S, tb), 0).astype(F32)
    for h in range(PEER_HEADS):
        for c in range(2):
            col = (2 * h + c) * PEER_HALF
            sc = _dot_nt(sk_ref[h, c], q_ref[:, col:col + PEER_HALF], precision=HIGHEST)
            for i in range(K):
                m = jnp.max(sc, axis=0, keepdims=True)
                idx = jnp.min(jnp.where(sc == m, key_iota, float(PEER_NKEYS)), axis=0, keepdims=True)
                v_ref[c, i:i + 1, :] = m
                i_ref[c, i:i + 1, :] = idx
                sc = jnp.where(key_iota == idx, -jnp.inf, sc)
        v2lo, v2hi = v_ref[1, 0:S, :], v_ref[1, S:K, :]
        i2lo, i2hi = i_ref[1, 0:S, :], i_ref[1, S:K, :]
        cands, eids, flats = [], [], []
        for a in range(S):
            cands.append(v_ref[0, a:a + 1, :] + v2lo)
            eids.append(i_ref[0, a:a + 1, :] * PEER_NKEYS + i2lo)
            flats.append(a * K + sub)
        cands.append(v_ref[0, 0:1, :] + v2hi)
        eids.append(i_ref[0, 0:1, :] * PEER_NKEYS + i2hi)
        flats.append(S + sub)
        cands.append(v_ref[0, S:K, :] + v_ref[1, 0:1, :])
        eids.append(i_ref[0, S:K, :] * PEER_NKEYS + i_ref[1, 0:1, :])
        flats.append((S + sub) * K)
        for i in range(K):
            m = functools.reduce(jnp.maximum, cands)
            m = jnp.max(m, axis=0, keepdims=True)
            pos = functools.reduce(jnp.minimum,
                                   [jnp.where(cd == m, fl, float(K * K)) for cd, fl in zip(cands, flats)])
            pos = jnp.min(pos, axis=0, keepdims=True)
            hits = [fl == pos for fl in flats]
            e = functools.reduce(jnp.add, [jnp.where(ht, ei, 0.0) for ht, ei in zip(hits, eids)])
            etop_ref[i:i + 1, :] = jnp.sum(e, axis=0, keepdims=True)
            top_ref[i:i + 1, :] = m
            cands = [jnp.where(ht, -jnp.inf, cd) for ht, cd in zip(hits, cands)]
        top_s = top_ref[...]
        p = jnp.exp(top_s - top_s[0:1, :])
        g_ref[h * K:(h + 1) * K, :] = p / jnp.sum(p, axis=0, keepdims=True)
        e_ref[h * K:(h + 1) * K, :] = etop_ref[...].astype(jnp.int32)


def _peer_topk(q, subkeys, tb):
    t, d = q.shape
    return pl.pallas_call(
        _peer_topk_kernel,
        out_shape=(jax.ShapeDtypeStruct((PEER_HK, t), jnp.int32),
                   jax.ShapeDtypeStruct((PEER_HK, t), F32)),
        grid=(t // tb,),
        in_specs=[pl.BlockSpec((tb, d), lambda i: (i, 0)),
                  pl.BlockSpec(subkeys.shape, lambda i: (0, 0, 0, 0))],
        out_specs=(pl.BlockSpec((PEER_HK, tb), lambda i: (0, i)),
                   pl.BlockSpec((PEER_HK, tb), lambda i: (0, i))),
        scratch_shapes=[pltpu.VMEM((2, PEER_TOPK, tb), F32),
                        pltpu.VMEM((2, PEER_TOPK, tb), F32),
                        pltpu.VMEM((PEER_TOPK, tb), F32),
                        pltpu.VMEM((PEER_TOPK, tb), F32)],
        compiler_params=_cparams("arbitrary"),
        name="peer_topk",
    )(q, subkeys)


def _gelu(x):
    return 0.5 * x * (1.0 + lax.erf(x * (1.0 / math.sqrt(2.0))))


def _unpack_bf16_pair(w):
    lo = lax.bitcast_convert_type(w << 16, F32)
    hi = lax.bitcast_convert_type(w & jnp.uint32(0xFFFF0000), F32)
    return lo, hi


def _peer_experts_kernel(e0_ref, e1_ref, e2_ref, x_ref, gates_ref, fg_ref, og_ref, uv_hbm,
                         o_ref, buf0, buf1, buf2, sem):
    i = pl.program_id(0)
    n = pl.num_programs(0)
    tpb = x_ref.shape[0]
    nw = x_ref.shape[1] // (2 * LANES)
    rows = tpb * PEER_HK
    pitch = PEER_ROW_PITCH
    bufs = (buf0, buf1, buf2)
    nbuf = len(bufs)

    def issue_token(e_ref, j, dst, dst_sem):
        for k in range(PEER_HK):
            r = j * PEER_HK + k
            e = e_ref[r]
            src = uv_hbm.at[pl.ds(pl.multiple_of(e * (2 * nw), 2 * nw), 2 * nw), :]
            pltpu.make_async_copy(src, dst.at[pl.ds(r * pitch, 2 * nw), :], dst_sem).start(priority=k % 2)

    def wait_all(b):
        window = pl.ds(0, rows * 2 * nw)
        pltpu.make_async_copy(bufs[(b + 1) % nbuf].at[window, :], bufs[b].at[window, :], sem.at[b]).wait()

    def chunk(buf, j, s):
        return buf[pl.ds(j * PEER_HK * pitch + s, PEER_HK, stride=pitch), :]

    @pl.when(i == 0)
    def _():
        for j in range(tpb):
            issue_token(e0_ref, j, buf0, sem.at[0])
            issue_token(e1_ref, j, buf1, sem.at[1])

    def step(b):
        ahead = (b + 2) % nbuf
        cur, nxt = bufs[b], bufs[ahead]
        wait_all(b)
        x = x_ref[...]
        hx = _rms(x, fg_ref[...])
        gates = gates_ref[...]
        eye = (lax.broadcasted_iota(jnp.int32, (PEER_HK, PEER_HK), 0)
               == lax.broadcasted_iota(jnp.int32, (PEER_HK, PEER_HK), 1))
        outs = []
        for j in range(tpb):
            acc = jnp.zeros((PEER_HK, LANES), F32)
            for s in range(nw):
                lo, hi = _unpack_bf16_pair(chunk(cur, j, s))
                acc = acc + lo * hx[j:j + 1, s * LANES:(s + 1) * LANES]
                acc = acc + hi * hx[j:j + 1, (nw + s) * LANES:(nw + s + 1) * LANES]
            pre = jnp.sum(acc, axis=1, keepdims=True)
            gcol = jnp.sum(jnp.where(eye, gates[j:j + 1, :], 0.0), axis=1, keepdims=True)
            act = _gelu(pre) * gcol
            los, his = [], []
            for s in range(nw):
                lo, hi = _unpack_bf16_pair(chunk(cur, j, nw + s))
                los.append(jnp.sum(lo * act, axis=0, keepdims=True))
                his.append(jnp.sum(hi * act, axis=0, keepdims=True))
            outs.append(jnp.concatenate(los + his, axis=1))
            issue_token(e2_ref, j, nxt, sem.at[ahead])
        x2 = x + jnp.concatenate(outs, axis=0)
        o_ref[...] = _rms(x2, og_ref[...])

        @pl.when(i == n - 1)
        def _():
            wait_all((b + 1) % nbuf)
            wait_all(ahead)

    for b in range(nbuf):
        pl.when(lax.rem(i, nbuf) == b)(functools.partial(step, b))


def _peer_experts(x1, eidx, gates, ffn_gain, out_gain, uv_tab):
    t, d = x1.shape
    tpb = PEER_TOKENS_PER_STEP
    n = t // tpb
    rows = tpb * PEER_HK
    nw = d // (2 * LANES)
    ids = eidx.reshape(t * PEER_HK)
    smem_blk = lambda ahead: pl.BlockSpec((rows,), lambda i: (jnp.minimum(i + ahead, n - 1),),
                                          memory_space=pltpu.SMEM)
    buf = pltpu.VMEM((rows * PEER_ROW_PITCH, LANES), jnp.uint32)
    return pl.pallas_call(
        _peer_experts_kernel,
        out_shape=jax.ShapeDtypeStruct((t, d), F32),
        grid=(n,),
        in_specs=[smem_blk(0), smem_blk(1), smem_blk(2),
                  pl.BlockSpec((tpb, d), lambda i: (i, 0)),
                  pl.BlockSpec((tpb, PEER_HK), lambda i: (i, 0)),
                  pl.BlockSpec((1, d), lambda i: (0, 0)),
                  pl.BlockSpec((1, d), lambda i: (0, 0)),
                  pl.BlockSpec(memory_space=pl.ANY)],
        out_specs=pl.BlockSpec((tpb, d), lambda i: (i, 0)),
        scratch_shapes=[buf, buf, buf, pltpu.SemaphoreType.DMA((3,))],
        compiler_params=_cparams("arbitrary"),
        name="peer_experts",
    )(ids, ids, ids, x1, gates, ffn_gain.reshape(1, d), out_gain.reshape(1, d), uv_tab)


def _pack_tables_kernel(u_ref, v_ref, o_ref):
    eb, d = u_ref.shape
    nw = d // (2 * LANES)

    def bf16_bits(x):
        return lax.bitcast_convert_type(x.astype(BF16).astype(F32), jnp.uint32)

    for t, ref in enumerate((u_ref, v_ref)):
        for s in range(nw):
            lo = bf16_bits(ref[:, s * LANES:(s + 1) * LANES]) >> 16
            hi = bf16_bits(ref[:, (nw + s) * LANES:(nw + s + 1) * LANES])
            o_ref[pl.ds(t * nw + s, eb, stride=2 * nw), :] = hi | lo


def _merge_expert_tables(u_tab, v_tab):
    ne, d = u_tab.shape
    nw = d // (2 * LANES)
    eb = 256
    return pl.pallas_call(
        _pack_tables_kernel,
        out_shape=jax.ShapeDtypeStruct((ne * 2 * nw, LANES), jnp.uint32),
        grid=(ne // eb,),
        in_specs=[pl.BlockSpec((eb, d), lambda i: (i, 0)),
                  pl.BlockSpec((eb, d), lambda i: (i, 0))],
        out_specs=pl.BlockSpec((eb * 2 * nw, LANES), lambda i: (i, 0)),
        compiler_params=_cparams("arbitrary"),
        name="pack_tables",
    )(u_tab, v_tab)


def _layer(x2d, bsz, seq, attn_norm, w_in, conv_w, conv_b, dt_bias, a_log, d_skip, rel_bias,
           attn_out_norm, ssm_out_norm, w_out, ffn_norm, peer_wq, peer_subkeys, peer_u, peer_v,
           out_gain, buckets, far_bucket):
    n_main = 3 * ATT_WIDTH + 2 * SSM_WIDTH + 2 * SSM_GROUPS * SSM_STATE
    w_main = w_in[:, :n_main].astype(BF16)
    w_dt = jnp.zeros((w_in.shape[0], LANES), F32).at[:, :SSM_HEADS].set(w_in[:, n_main:])
    proj, dt, dtt = _in_proj(x2d, attn_norm, w_main, w_dt, 512, 512)
    tiles = _bias_tiles(rel_bias, buckets)
    att = _moba(proj, rel_bias, tiles, far_bucket, bsz, seq)
    y = _ssd(proj, dt, dtt, conv_w, conv_b, dt_bias, a_log, d_skip, ssm_out_norm, bsz, seq)
    x1 = _out_proj(att, y, attn_out_norm, x2d, w_out.astype(BF16), 512, 1024)
    pq = _norm_matmul(x1, ffn_norm, peer_wq.astype(BF16), 512, 1024)
    e_t, g_t = _peer_topk(pq, peer_subkeys, LANES)
    uv = _merge_expert_tables(peer_u, peer_v)
    return _peer_experts(x1, e_t.T, g_t.T, ffn_norm, out_gain, uv)


def kernel(x, attn_norm, w_in, conv_w, conv_b, dt_bias, a_log, d_skip, rel_bias, attn_out_norm,
           ssm_out_norm, w_out, ffn_norm, peer_wq, peer_subkeys, peer_u, peer_v, final_norm):
    bsz, seq, d = x.shape
    depth = w_in.shape[0]
    assert depth == 1, "the final rmsnorm is fused into the last layer's expert kernel"
    buckets_np, far_bucket = _bucket_tiles(seq)
    buckets = jnp.asarray(buckets_np)
    out = _layer(x.reshape(bsz * seq, d), bsz, seq, attn_norm[0], w_in[0], conv_w[0], conv_b[0],
                 dt_bias[0], a_log[0], d_skip[0], rel_bias, attn_out_norm[0], ssm_out_norm[0],
                 w_out[0], ffn_norm[0], peer_wq[0], peer_subkeys[0], peer_u[0], peer_v[0],
                 final_norm, buckets, far_bucket)
    return out.reshape(bsz, seq, d)
```

```python
import functools
import math

import numpy as np
import jax
import jax.numpy as jnp
from jax import lax
from jax.experimental import pallas as pl
from jax.experimental.pallas import tpu as pltpu

F32 = jnp.float32
BF16 = jnp.bfloat16
EPS = 1e-6
NEG = -1e30

LANES = 128
SUBLANES = 8
VMEM_LIMIT = 56 * 1024 * 1024

ATT_HEADS = 8
ATT_HEAD_DIM = 128
ATT_WIDTH = ATT_HEADS * ATT_HEAD_DIM
MOBA_BLOCK = 256
MOBA_TOPK = 3
MOBA_HEADS_PER_STEP = 2
REL_BUCKETS = 32
REL_MAX_DIST = 128
SSM_HEADS = 16
SSM_HEAD_DIM = 64
SSM_WIDTH = SSM_HEADS * SSM_HEAD_DIM
SSM_GROUPS = 2
SSM_STATE = 128
SSM_CONV = 4
SSD_CHUNK = 128
PEER_HEADS = 8
PEER_NKEYS = 128
PEER_HALF = 128
PEER_TOPK = 16
PEER_HK = PEER_HEADS * PEER_TOPK
PEER_TOKENS_PER_STEP = 16
PEER_RING_TOKENS = 8
PEER_ROW_PITCH = 20

HIGHEST = lax.Precision.HIGHEST


def _cparams(*sem):
    return pltpu.CompilerParams(dimension_semantics=sem, vmem_limit_bytes=VMEM_LIMIT)


def _rms(x, gain):
    return x * lax.rsqrt(jnp.mean(x * x, axis=-1, keepdims=True) + EPS) * gain


def _dot_nt(a, b, precision=None):
    return lax.dot_general(a, b, (((1,), (1,)), ((), ())), precision=precision,
                           preferred_element_type=F32)


def _norm_matmul_kernel(x_ref, g_ref, w_ref, o_ref, hn_ref):
    @pl.when(pl.program_id(1) == 0)
    def _():
        hn_ref[...] = _rms(x_ref[...], g_ref[...]).astype(BF16)

    o_ref[...] = jnp.dot(hn_ref[...], w_ref[...], preferred_element_type=F32)


def _norm_matmul(x, gain, w, tm, tn):
    t, d = x.shape
    n = w.shape[1]
    return pl.pallas_call(
        _norm_matmul_kernel,
        out_shape=jax.ShapeDtypeStruct((t, n), F32),
        grid=(t // tm, n // tn),
        in_specs=[pl.BlockSpec((tm, d), lambda i, j: (i, 0)),
                  pl.BlockSpec((1, d), lambda i, j: (0, 0)),
                  pl.BlockSpec((d, tn), lambda i, j: (0, j))],
        out_specs=pl.BlockSpec((tm, tn), lambda i, j: (i, j)),
        scratch_shapes=[pltpu.VMEM((tm, d), BF16)],
        compiler_params=_cparams("arbitrary", "arbitrary"),
        name="norm_matmul",
    )(x, gain.reshape(1, d), w)


def _in_proj_kernel(x_ref, g_ref, w_ref, wdt_ref, o_ref, dt_ref, dtt_ref, hn_ref):
    @pl.when(pl.program_id(1) == 0)
    def _():
        hn = _rms(x_ref[...], g_ref[...])
        hn_ref[...] = hn.astype(BF16)
        dt = jnp.dot(hn, wdt_ref[...], precision=HIGHEST, preferred_element_type=F32)
        dt_ref[...] = dt
        dtt_ref[...] = dt.T

    o_ref[...] = jnp.dot(hn_ref[...], w_ref[...], preferred_element_type=F32)


def _in_proj(x, gain, w_main, w_dt, tm, tn):
    t, d = x.shape
    n = w_main.shape[1]
    return pl.pallas_call(
        _in_proj_kernel,
        out_shape=(jax.ShapeDtypeStruct((t, n), F32),
                   jax.ShapeDtypeStruct((t, LANES), F32),
                   jax.ShapeDtypeStruct((LANES, t), F32)),
        grid=(t // tm, n // tn),
        in_specs=[pl.BlockSpec((tm, d), lambda i, j: (i, 0)),
                  pl.BlockSpec((1, d), lambda i, j: (0, 0)),
                  pl.BlockSpec((d, tn), lambda i, j: (0, j)),
                  pl.BlockSpec((d, LANES), lambda i, j: (0, 0))],
        out_specs=(pl.BlockSpec((tm, tn), lambda i, j: (i, j)),
                   pl.BlockSpec((tm, LANES), lambda i, j: (i, 0)),
                   pl.BlockSpec((LANES, tm), lambda i, j: (0, i))),
        scratch_shapes=[pltpu.VMEM((tm, d), BF16)],
        compiler_params=_cparams("arbitrary", "arbitrary"),
        name="in_proj",
    )(x, gain.reshape(1, d), w_main, w_dt)


def _t5_bucket_np(rel):
    n = np.maximum(rel, 0)
    max_exact = REL_BUCKETS // 2
    nf = np.maximum(n, 1).astype(np.float64)
    large = max_exact + (np.log(nf / max_exact) / math.log(REL_MAX_DIST / max_exact)
                         * (REL_BUCKETS - max_exact)).astype(np.int32)
    large = np.minimum(large, REL_BUCKETS - 1)
    return np.where(n < max_exact, n, large).astype(np.int32)


def _bucket_tiles(seq):
    qi = np.arange(MOBA_BLOCK)[:, None]
    ki = np.arange(MOBA_BLOCK)[None, :]
    own = _t5_bucket_np(qi - ki)
    prev = _t5_bucket_np(MOBA_BLOCK + qi - ki)
    far = _t5_bucket_np(np.arange(MOBA_BLOCK + 1, max(seq, MOBA_BLOCK + 2)))
    assert (far == far[0]).all(), "bias must be constant beyond the previous block"
    return np.stack([own, prev]), int(far[0])


def _bias_tiles_kernel(rb_ref, bk_ref, o_ref):
    h = pl.program_id(0)
    bk = bk_ref[...]
    acc = jnp.zeros(bk.shape, F32)
    for b in range(REL_BUCKETS):
        acc = jnp.where(bk == b, rb_ref[b, h], acc)
    o_ref[0] = acc


def _bias_tiles(rel_bias, buckets):
    nh = rel_bias.shape[1]
    return pl.pallas_call(
        _bias_tiles_kernel,
        out_shape=jax.ShapeDtypeStruct((nh, 2, MOBA_BLOCK, MOBA_BLOCK), F32),
        grid=(nh,),
        in_specs=[pl.BlockSpec(memory_space=pltpu.SMEM),
                  pl.BlockSpec((2, MOBA_BLOCK, MOBA_BLOCK), lambda h: (0, 0, 0))],
        out_specs=pl.BlockSpec((1, 2, MOBA_BLOCK, MOBA_BLOCK), lambda h: (h, 0, 0, 0)),
        compiler_params=_cparams("arbitrary"),
        name="bias_tiles",
    )(rel_bias, buckets)


def _moba_kernel(far_bucket, rb_ref, q_ref, k_ref, v_ref, bias_ref, o_ref, kmean_ref, kb_ref, vb_ref):
    hg = pl.program_id(1)
    c = pl.program_id(2)
    nh = kmean_ref.shape[0]
    nb = kmean_ref.shape[1]
    blk = MOBA_BLOCK
    hd = ATT_HEAD_DIM
    scale = hd ** -0.5

    @pl.when(c == 0)
    def _():
        k = k_ref[...]
        kb_ref[...] = k.astype(BF16)
        vb_ref[...] = v_ref[...].astype(BF16)
        for hh in range(nh):
            kk = k[:, hh * hd:(hh + 1) * hd].reshape(nb, blk, hd)
            kmean_ref[hh] = jnp.sum(kk, axis=1) * (1.0 / blk)

    blk_id = lax.broadcasted_iota(jnp.int32, (nb, blk), 0)
    past = blk_id < c
    row = lax.broadcasted_iota(jnp.int32, (blk, blk), 0)
    colk = lax.broadcasted_iota(jnp.int32, (blk, blk), 1)
    own_off = pl.multiple_of(c * blk, blk)
    pair_blk = lax.broadcasted_iota(jnp.int32, (nb, 2 * blk), 0)
    pair_half = jnp.where(lax.broadcasted_iota(jnp.int32, (nb, 2 * blk), 1) >= blk, 1, 0)

    qbs, sels, fars, init = [], [], [], []
    for hh in range(nh):
        hs = slice(hh * hd, (hh + 1) * hd)
        q = q_ref[:, hs]
        gate = _dot_nt(kmean_ref[hh], q, precision=HIGHEST)
        gm = jnp.where(past, gate, -jnp.inf)
        rank = jnp.zeros(gate.shape, jnp.int32)
        for j in range(nb):
            gj = gm[j:j + 1, :]
            beats = (gj > gm) | ((gj == gm) & (j < blk_id))
            rank = rank + beats.astype(jnp.int32)
        sels.append(jnp.where(past & (rank < MOBA_TOPK), 1.0, 0.0))
        qb = q.astype(BF16)
        qbs.append(qb)
        fars.append(rb_ref[far_bucket, hg * nh + hh])
        s = _dot_nt(qb, kb_ref[pl.ds(own_off, blk), hs]) * scale + bias_ref[hh, 0]
        s = jnp.where(row >= colk, s, NEG)
        m0 = jnp.max(s, axis=1, keepdims=True)
        p = jnp.exp(s - m0)
        l0 = jnp.sum(p, axis=1, keepdims=True)
        acc0 = jnp.dot(p.astype(BF16), vb_ref[pl.ds(own_off, blk), hs], preferred_element_type=F32)
        init.append((m0, l0, acc0))

    def body(jp, carry):
        off = pl.multiple_of(jp * (2 * blk), 2 * blk)
        pick = jnp.where(pair_blk == 2 * jp + pair_half, 1.0, 0.0)
        out = []
        for hh in range(nh):
            hs = slice(hh * hd, (hh + 1) * hd)
            m, l, acc = carry[hh]
            sj = _dot_nt(qbs[hh], kb_ref[pl.ds(off, 2 * blk), hs]) * scale
            chosen = lax.dot_general(sels[hh], pick, (((0,), (0,)), ((), ())),
                                     preferred_element_type=F32)
            bias = jnp.concatenate([jnp.where(2 * jp + u == c - 1, bias_ref[hh, 1], fars[hh])
                                    for u in range(2)], axis=1)
            sj = jnp.where(chosen > 0.0, sj + bias, NEG)
            m_new = jnp.maximum(m, jnp.max(sj, axis=1, keepdims=True))
            alpha = jnp.exp(m - m_new)
            pj = jnp.exp(sj - m_new)
            l_new = alpha * l + jnp.sum(pj, axis=1, keepdims=True)
            pv = jnp.dot(pj.astype(BF16), vb_ref[pl.ds(off, 2 * blk), hs], preferred_element_type=F32)
            out.append((m_new, l_new, alpha * acc + pv))
        return tuple(out)

    final = lax.fori_loop(0, (c + 1) // 2, body, tuple(init))
    for hh in range(nh):
        _, l, acc = final[hh]
        o_ref[:, hh * hd:(hh + 1) * hd] = acc / l


def _moba(proj, rel_bias, bias_tiles, far_bucket, bsz, seq):
    nb = seq // MOBA_BLOCK
    t = bsz * seq
    nh = MOBA_HEADS_PER_STEP
    gw = nh * ATT_HEAD_DIM
    ng = ATT_HEADS // nh
    return pl.pallas_call(
        functools.partial(_moba_kernel, far_bucket),
        out_shape=jax.ShapeDtypeStruct((t, ATT_WIDTH), F32),
        grid=(bsz, ng, nb),
        in_specs=[pl.BlockSpec(memory_space=pltpu.SMEM),
                  pl.BlockSpec((MOBA_BLOCK, gw), lambda b, g, c: (b * nb + c, g)),
                  pl.BlockSpec((seq, gw), lambda b, g, c: (b, ng + g)),
                  pl.BlockSpec((seq, gw), lambda b, g, c: (b, 2 * ng + g)),
                  pl.BlockSpec((nh, 2, MOBA_BLOCK, MOBA_BLOCK), lambda b, g, c: (g, 0, 0, 0))],
        out_specs=pl.BlockSpec((MOBA_BLOCK, gw), lambda b, g, c: (b * nb + c, g)),
        scratch_shapes=[pltpu.VMEM((nh, nb, ATT_HEAD_DIM), F32),
                        pltpu.VMEM((seq, gw), BF16),
                        pltpu.VMEM((seq, gw), BF16)],
        compiler_params=_cparams("arbitrary", "arbitrary", "arbitrary"),
        name="moba",
    )(rel_bias, proj, proj, proj, bias_tiles)


def _silu(x):
    return x / (1.0 + jnp.exp(-x))


def _softplus(x):
    return jnp.maximum(x, 0.0) + jnp.log(1.0 + jnp.exp(-jnp.abs(x)))


def _ssd_kernel(z_ref, xs_ref, bc_ref, dt_ref, dtt_ref, cwx_ref, cwbc_ref, cbx_ref, cbbc_ref,
                dtb_ref, dtbt_ref, a_ref, at_ref, dsk_ref, gain_ref, o_ref,
                extx_ref, extbc_ref, st_ref, y_ref):
    c = pl.program_id(1)
    L = SSD_CHUNK
    P = SSM_HEAD_DIM
    N = SSM_STATE
    hpg = SSM_HEADS // SSM_GROUPS

    @pl.when(c == 0)
    def _():
        extx_ref[0:SUBLANES, :] = jnp.zeros((SUBLANES, SSM_WIDTH), F32)
        extbc_ref[0:SUBLANES, :] = jnp.zeros((SUBLANES, 2 * SSM_GROUPS * N), F32)
        st_ref[...] = jnp.zeros(st_ref.shape, F32)

    extx_ref[SUBLANES:SUBLANES + L, :] = xs_ref[...]
    extbc_ref[SUBLANES:SUBLANES + L, :] = bc_ref[...]

    def conv(ext_ref, w_ref, b_ref):
        acc = b_ref[...]
        for i in range(SSM_CONV):
            acc = acc + ext_ref[SUBLANES - i:SUBLANES - i + L, :] * w_ref[SSM_CONV - 1 - i:SSM_CONV - i, :]
        return _silu(acc)

    xs = conv(extx_ref, cwx_ref, cbx_ref)
    bcm = conv(extbc_ref, cwbc_ref, cbbc_ref)
    extx_ref[0:SUBLANES, :] = extx_ref[L:L + SUBLANES, :]
    extbc_ref[0:SUBLANES, :] = extbc_ref[L:L + SUBLANES, :]

    dt = _softplus(dt_ref[...] + dtb_ref[...])
    dtt = _softplus(dtt_ref[...] + dtbt_ref[...])
    da = dt * (-jnp.exp(a_ref[...]))
    dat = dtt * (-jnp.exp(at_ref[...]))
    r = lax.broadcasted_iota(jnp.int32, (L, L), 0)
    s_ = lax.broadcasted_iota(jnp.int32, (L, L), 1)
    causal = r >= s_
    tril = jnp.where(causal, 1.0, 0.0)
    triu = jnp.where(r <= s_, 1.0, 0.0)
    cs = jnp.dot(tril, da, precision=HIGHEST, preferred_element_type=F32)
    cst = jnp.dot(dat, triu, precision=HIGHEST, preferred_element_type=F32)
    ecs = jnp.exp(cs)
    cs_last = cs[L - 1:L, :]
    dst = jnp.exp(cs_last - cs)
    cdec = jnp.exp(cs_last)

    for g in range(SSM_GROUPS):
        bg = bcm[:, g * N:(g + 1) * N]
        cg = bcm[:, (SSM_GROUPS + g) * N:(SSM_GROUPS + g + 1) * N]
        bgb = bg.astype(BF16)
        cgb = cg.astype(BF16)
        cb = _dot_nt(cgb, bgb)
        bgt = jnp.transpose(bg).astype(BF16)
        for hh in range(hpg):
            h = g * hpg + hh
            xh = xs[:, h * P:(h + 1) * P]
            xdt = xh * dt[:, h:h + 1]
            seg = cs[:, h:h + 1] - cst[h:h + 1, :]
            decay = jnp.exp(jnp.where(causal, seg, NEG))
            y = jnp.dot((cb * decay).astype(BF16), xdt.astype(BF16), preferred_element_type=F32)
            st = st_ref[h]
            y = y + jnp.dot(cgb, st.astype(BF16), preferred_element_type=F32) * ecs[:, h:h + 1]
            y = y + xh * dsk_ref[:, h:h + 1]
            y_ref[:, h * P:(h + 1) * P] = y
            new = jnp.dot(bgt, (xdt * dst[:, h:h + 1]).astype(BF16), preferred_element_type=F32)
            st_ref[h] = st * cdec[:, h:h + 1] + new

    y = y_ref[...] * _silu(z_ref[...])
    gw = SSM_WIDTH // SSM_GROUPS
    for g in range(SSM_GROUPS):
        yg = y[:, g * gw:(g + 1) * gw]
        o_ref[:, g * gw:(g + 1) * gw] = _rms(yg, gain_ref[:, g * gw:(g + 1) * gw]).astype(o_ref.dtype)


def _pad_lanes(v):
    return jnp.zeros((1, LANES), F32).at[0, :v.shape[0]].set(v.astype(F32))


def _ssd(proj, dt, dtt, conv_w, conv_b, dt_bias, a_log, d_skip, gain, bsz, seq):
    t = bsz * seq
    nc = seq // SSD_CHUNK
    L = SSD_CHUNK
    nbc = 2 * SSM_GROUPS * SSM_STATE
    zcol = 3 * ATT_WIDTH // SSM_WIDTH
    xcol = zcol + 1
    bccol = (3 * ATT_WIDTH + 2 * SSM_WIDTH) // nbc
    dtb = _pad_lanes(dt_bias)
    alog = jnp.full((1, LANES), -jnp.inf, F32).at[0, :SSM_HEADS].set(a_log.astype(F32))
    dsk = _pad_lanes(d_skip)
    full = lambda shape: pl.BlockSpec(shape, lambda b, c: tuple(0 for _ in shape))
    return pl.pallas_call(
        _ssd_kernel,
        out_shape=jax.ShapeDtypeStruct((t, SSM_WIDTH), BF16),
        grid=(bsz, nc),
        in_specs=[pl.BlockSpec((L, SSM_WIDTH), lambda b, c: (b * nc + c, zcol)),
                  pl.BlockSpec((L, SSM_WIDTH), lambda b, c: (b * nc + c, xcol)),
                  pl.BlockSpec((L, nbc), lambda b, c: (b * nc + c, bccol)),
                  pl.BlockSpec((L, LANES), lambda b, c: (b * nc + c, 0)),
                  pl.BlockSpec((LANES, L), lambda b, c: (0, b * nc + c)),
                  full((SSM_CONV, SSM_WIDTH)), full((SSM_CONV, nbc)),
                  full((1, SSM_WIDTH)), full((1, nbc)),
                  full((1, LANES)), full((LANES, 1)), full((1, LANES)), full((LANES, 1)),
                  full((1, LANES)), full((1, SSM_WIDTH))],
        out_specs=pl.BlockSpec((L, SSM_WIDTH), lambda b, c: (b * nc + c, 0)),
        scratch_shapes=[pltpu.VMEM((L + 2 * SUBLANES, SSM_WIDTH), F32),
                        pltpu.VMEM((L + 2 * SUBLANES, nbc), F32),
                        pltpu.VMEM((SSM_HEADS, SSM_STATE, SSM_HEAD_DIM), F32),
                        pltpu.VMEM((L, SSM_WIDTH), F32)],
        compiler_params=_cparams("arbitrary", "arbitrary"),
        name="ssd",
    )(proj, proj, proj, dt, dtt,
      conv_w[:, :SSM_WIDTH], conv_w[:, SSM_WIDTH:],
      conv_b[:SSM_WIDTH].reshape(1, -1), conv_b[SSM_WIDTH:].reshape(1, -1),
      dtb, dtb.reshape(LANES, 1), alog, alog.reshape(LANES, 1), dsk, gain.reshape(1, -1))


def _out_proj_kernel(att_ref, y_ref, g_ref, x_ref, w_ref, o_ref, mix_ref):
    @pl.when(pl.program_id(1) == 0)
    def _():
        mix_ref[:, :ATT_WIDTH] = _rms(att_ref[...], g_ref[...]).astype(BF16)
        mix_ref[:, ATT_WIDTH:] = y_ref[...]

    o_ref[...] = x_ref[...] + jnp.dot(mix_ref[...], w_ref[...], preferred_element_type=F32)


def _out_proj(att, y, gain, x, w, tm, tn):
    t, d = x.shape
    k = w.shape[0]
    return pl.pallas_call(
        _out_proj_kernel,
        out_shape=jax.ShapeDtypeStruct((t, d), F32),
        grid=(t // tm, d // tn),
        in_specs=[pl.BlockSpec((tm, ATT_WIDTH), lambda i, j: (i, 0)),
                  pl.BlockSpec((tm, SSM_WIDTH), lambda i, j: (i, 0)),
                  pl.BlockSpec((1, ATT_WIDTH), lambda i, j: (0, 0)),
                  pl.BlockSpec((tm, tn), lambda i, j: (i, j)),
                  pl.BlockSpec((k, tn), lambda i, j: (0, j))],
        out_specs=pl.BlockSpec((tm, tn), lambda i, j: (i, j)),
        scratch_shapes=[pltpu.VMEM((tm, k), BF16)],
        compiler_params=_cparams("arbitrary", "arbitrary"),
        name="out_proj",
    )(att, y, gain.reshape(1, -1), x, w)


def _peer_topk_kernel(q_ref, sk_ref, e_ref, g_ref, v_ref, i_ref, top_ref, etop_ref):
    tb = q_ref.shape[0]
    K = PEER_TOPK
    S = SUBLANES
    key_iota = lax.broadcasted_iota(jnp.int32, (PEER_NKEYS, tb), 0).astype(F32)
    sub = lax.broadcasted_iota(jnp.int32, (S, tb), 0).astype(F32)
    for h in range(PEER_HEADS):
        for c in range(2):
            col = (2 * h + c) * PEER_HALF
            sc = _dot_nt(sk_ref[h, c], q_ref[:, col:col + PEER_HALF], precision=HIGHEST)
            for i in range(K):
                m = jnp.max(sc, axis=0, keepdims=True)
                idx = jnp.min(jnp.where(sc == m, key_iota, float(PEER_NKEYS)), axis=0, keepdims=True)
                v_ref[c, i:i + 1, :] = m
                i_ref[c, i:i + 1, :] = idx
                sc = jnp.where(key_iota == idx, -jnp.inf, sc)
        v2lo, v2hi = v_ref[1, 0:S, :], v_ref[1, S:K, :]
        i2lo, i2hi = i_ref[1, 0:S, :], i_ref[1, S:K, :]
        cands, eids, flats = [], [], []
        for a in range(S):
            cands.append(v_ref[0, a:a + 1, :] + v2lo)
            eids.append(i_ref[0, a:a + 1, :] * PEER_NKEYS + i2lo)
            flats.append(a * K + sub)
        cands.append(v_ref[0, 0:1, :] + v2hi)
        eids.append(i_ref[0, 0:1, :] * PEER_NKEYS + i2hi)
        flats.append(S + sub)
        cands.append(v_ref[0, S:K, :] + v_ref[1, 0:1, :])
        eids.append(i_ref[0, S:K, :] * PEER_NKEYS + i_ref[1, 0:1, :])
        flats.append((S + sub) * K)
        for i in range(K):
            m = functools.reduce(jnp.maximum, cands)
            m = jnp.max(m, axis=0, keepdims=True)
            pos = functools.reduce(jnp.minimum,
                                   [jnp.where(cd == m, fl, float(K * K)) for cd, fl in zip(cands, flats)])
            pos = jnp.min(pos, axis=0, keepdims=True)
            hits = [fl == pos for fl in flats]
            e = functools.reduce(jnp.add, [jnp.where(ht, ei, 0.0) for ht, ei in zip(hits, eids)])
            etop_ref[i:i + 1, :] = jnp.sum(e, axis=0, keepdims=True)
            top_ref[i:i + 1, :] = m
            cands = [jnp.where(ht, -jnp.inf, cd) for ht, cd in zip(hits, cands)]
        top_s = top_ref[...]
        p = jnp.exp(top_s - top_s[0:1, :])
        g_ref[h * K:(h + 1) * K, :] = p / jnp.sum(p, axis=0, keepdims=True)
        e_ref[h * K:(h + 1) * K, :] = etop_ref[...].astype(jnp.int32)


def _peer_topk(q, subkeys, tb):
    t, d = q.shape
    return pl.pallas_call(
        _peer_topk_kernel,
        out_shape=(jax.ShapeDtypeStruct((PEER_HK, t), jnp.int32),
                   jax.ShapeDtypeStruct((PEER_HK, t), F32)),
        grid=(t // tb,),
        in_specs=[pl.BlockSpec((tb, d), lambda i: (i, 0)),
                  pl.BlockSpec(subkeys.shape, lambda i: (0, 0, 0, 0))],
        out_specs=(pl.BlockSpec((PEER_HK, tb), lambda i: (0, i)),
                   pl.BlockSpec((PEER_HK, tb), lambda i: (0, i))),
        scratch_shapes=[pltpu.VMEM((2, PEER_TOPK, tb), F32),
                        pltpu.VMEM((2, PEER_TOPK, tb), F32),
                        pltpu.VMEM((PEER_TOPK, tb), F32),
                        pltpu.VMEM((PEER_TOPK, tb), F32)],
        compiler_params=_cparams("arbitrary"),
        name="peer_topk",
    )(q, subkeys)


def _gelu(x):
    return 0.5 * x * (1.0 + lax.erf(x * (1.0 / math.sqrt(2.0))))


def _unpack_bf16_pair(w):
    lo = lax.bitcast_convert_type(w << 16, F32)
    hi = lax.bitcast_convert_type(w & jnp.uint32(0xFFFF0000), F32)
    return lo, hi


def _peer_experts_kernel(x_ref, gates_ref, fg_ref, og_ref, ids_hbm, uv_hbm, o_ref,
                         buf0, buf1, buf2, ids_smem, sem, ids_sem):
    i = pl.program_id(0)
    n = pl.num_programs(0)
    sub = PEER_RING_TOKENS
    nsub = x_ref.shape[0] // sub
    nw = x_ref.shape[1] // (2 * LANES)
    rows = sub * PEER_HK
    pitch = PEER_ROW_PITCH
    bufs = (buf0, buf1, buf2)
    nbuf = len(bufs)
    assert nsub == 2 and nbuf == 3, "the ring arithmetic below is written for 2 sub-blocks, 3 buffers"

    def ids_copy(step, slot):
        return pltpu.make_async_copy(ids_hbm.at[step], ids_smem.at[slot], ids_sem.at[slot])

    def issue_token(slot, u, j, dst, dst_sem):
        for k in range(PEER_HK):
            r = j * PEER_HK + k
            e = ids_smem[slot, 0, u * rows + r]
            src = uv_hbm.at[pl.ds(pl.multiple_of(e * (2 * nw), 2 * nw), 2 * nw), :]
            pltpu.make_async_copy(src, dst.at[pl.ds(r * pitch, 2 * nw), :], dst_sem).start(priority=k % 2)

    def wait_all(b):
        window = pl.ds(0, rows * 2 * nw)
        pltpu.make_async_copy(bufs[(b + 1) % nbuf].at[window, :], bufs[b].at[window, :], sem.at[b]).wait()

    def chunk(buf, j, s):
        return buf[pl.ds(j * PEER_HK * pitch + s, PEER_HK, stride=pitch), :]

    @pl.when(i == 0)
    def _():
        ids_copy(0, 0).start()
        ids_copy(jnp.minimum(1, n - 1), 2).start()
        ids_copy(0, 0).wait()
        for u in range(nsub):
            for j in range(sub):
                issue_token(0, u, j, bufs[u], sem.at[u])

    def step(p):
        nxt_slot = (p + 2) % nbuf
        ids_copy(0, nxt_slot).wait()
        ids_copy(jnp.minimum(i + 2, n - 1), (p + 1) % nbuf).start()
        x = x_ref[...]
        hx = _rms(x, fg_ref[...])
        gates = gates_ref[...]
        eye = (lax.broadcasted_iota(jnp.int32, (PEER_HK, PEER_HK), 0)
               == lax.broadcasted_iota(jnp.int32, (PEER_HK, PEER_HK), 1))
        outs = []
        for u in range(nsub):
            b = (p + u) % nbuf
            ahead = (b + 2) % nbuf
            cur = bufs[b]
            wait_all(b)
            for j in range(sub):
                t = u * sub + j
                acc = jnp.zeros((PEER_HK, LANES), F32)
                for s in range(nw):
                    lo, hi = _unpack_bf16_pair(chunk(cur, j, s))
                    acc = acc + lo * hx[t:t + 1, s * LANES:(s + 1) * LANES]
                    acc = acc + hi * hx[t:t + 1, (nw + s) * LANES:(nw + s + 1) * LANES]
                pre = jnp.sum(acc, axis=1, keepdims=True)
                gcol = jnp.sum(jnp.where(eye, gates[t:t + 1, :], 0.0), axis=1, keepdims=True)
                act = _gelu(pre) * gcol
                los, his = [], []
                for s in range(nw):
                    lo, hi = _unpack_bf16_pair(chunk(cur, j, nw + s))
                    los.append(jnp.sum(lo * act, axis=0, keepdims=True))
                    his.append(jnp.sum(hi * act, axis=0, keepdims=True))
                outs.append(jnp.concatenate(los + his, axis=1))
                issue_token(nxt_slot, u, j, bufs[ahead], sem.at[ahead])
        x2 = x + jnp.concatenate(outs, axis=0)
        o_ref[...] = _rms(x2, og_ref[...])

        @pl.when(i == n - 1)
        def _():
            wait_all((p + 2) % nbuf)
            wait_all(p)
            ids_copy(0, (p + 1) % nbuf).wait()

    for p in range(nbuf):
        pl.when(lax.rem(nsub * i, nbuf) == p)(functools.partial(step, p))


def _peer_experts(x1, eidx, gates, ffn_gain, out_gain, uv_tab):
    t, d = x1.shape
    tps = PEER_TOKENS_PER_STEP
    n = t // tps
    rows = PEER_RING_TOKENS * PEER_HK
    buf = pltpu.VMEM((rows * PEER_ROW_PITCH, LANES), jnp.uint32)
    return pl.pallas_call(
        _peer_experts_kernel,
        out_shape=jax.ShapeDtypeStruct((t, d), F32),
        grid=(n,),
        in_specs=[pl.BlockSpec((tps, d), lambda i: (i, 0)),
                  pl.BlockSpec((tps, PEER_HK), lambda i: (i, 0)),
                  pl.BlockSpec((1, d), lambda i: (0, 0)),
                  pl.BlockSpec((1, d), lambda i: (0, 0)),
                  pl.BlockSpec(memory_space=pl.ANY),
                  pl.BlockSpec(memory_space=pl.ANY)],
        out_specs=pl.BlockSpec((tps, d), lambda i: (i, 0)),
        scratch_shapes=[buf, buf, buf,
                        pltpu.SMEM((3, 1, tps * PEER_HK), jnp.int32),
                        pltpu.SemaphoreType.DMA((3,)),
                        pltpu.SemaphoreType.DMA((3,))],
        compiler_params=_cparams("arbitrary"),
        name="peer_experts",
    )(x1, gates, ffn_gain.reshape(1, d), out_gain.reshape(1, d), eidx.reshape(n, 1, tps * PEER_HK), uv_tab)


def _pack_tables_kernel(u_ref, v_ref, o_ref):
    eb, d = u_ref.shape
    nw = d // (2 * LANES)

    def bf16_bits(x):
        return lax.bitcast_convert_type(x.astype(BF16).astype(F32), jnp.uint32)

    for t, ref in enumerate((u_ref, v_ref)):
        for s in range(nw):
            lo = bf16_bits(ref[:, s * LANES:(s + 1) * LANES]) >> 16
            hi = bf16_bits(ref[:, (nw + s) * LANES:(nw + s + 1) * LANES])
            o_ref[pl.ds(t * nw + s, eb, stride=2 * nw), :] = hi | lo


def _merge_expert_tables(u_tab, v_tab):
    ne, d = u_tab.shape
    nw = d // (2 * LANES)
    eb = 256
    return pl.pallas_call(
        _pack_tables_kernel,
        out_shape=jax.ShapeDtypeStruct((ne * 2 * nw, LANES), jnp.uint32),
        grid=(ne // eb,),
        in_specs=[pl.BlockSpec((eb, d), lambda i: (i, 0)),
                  pl.BlockSpec((eb, d), lambda i: (i, 0))],
        out_specs=pl.BlockSpec((eb * 2 * nw, LANES), lambda i: (i, 0)),
        compiler_params=_cparams("arbitrary"),
        name="pack_tables",
    )(u_tab, v_tab)


def _layer(x2d, bsz, seq, attn_norm, w_in, conv_w, conv_b, dt_bias, a_log, d_skip, rel_bias,
           attn_out_norm, ssm_out_norm, w_out, ffn_norm, peer_wq, peer_subkeys, peer_u, peer_v,
           out_gain, buckets, far_bucket):
    n_main = 3 * ATT_WIDTH + 2 * SSM_WIDTH + 2 * SSM_GROUPS * SSM_STATE
    w_main = w_in[:, :n_main].astype(BF16)
    w_dt = jnp.zeros((w_in.shape[0], LANES), F32).at[:, :SSM_HEADS].set(w_in[:, n_main:])
    proj, dt, dtt = _in_proj(x2d, attn_norm, w_main, w_dt, 1024, 512)
    tiles = _bias_tiles(rel_bias, buckets)
    att = _moba(proj, rel_bias, tiles, far_bucket, bsz, seq)
    y = _ssd(proj, dt, dtt, conv_w, conv_b, dt_bias, a_log, d_skip, ssm_out_norm, bsz, seq)
    d_model = x2d.shape[1]
    x1 = _out_proj(att, y, attn_out_norm, x2d, w_out.astype(BF16), 512, d_model)
    pq = _norm_matmul(x1, ffn_norm, peer_wq.astype(BF16), 512, d_model)
    e_t, g_t = _peer_topk(pq, peer_subkeys, LANES)
    uv = _merge_expert_tables(peer_u, peer_v)
    return _peer_experts(x1, e_t.T, g_t.T, ffn_norm, out_gain, uv)


def kernel(x, attn_norm, w_in, conv_w, conv_b, dt_bias, a_log, d_skip, rel_bias, attn_out_norm,
           ssm_out_norm, w_out, ffn_norm, peer_wq, peer_subkeys, peer_u, peer_v, final_norm):
    bsz, seq, d = x.shape
    depth = w_in.shape[0]
    assert depth == 1, "the final rmsnorm is fused into the last layer's expert kernel"
    buckets_np, far_bucket = _bucket_tiles(seq)
    buckets = jnp.asarray(buckets_np)
    out = _layer(x.reshape(bsz * seq, d), bsz, seq, attn_norm[0], w_in[0], conv_w[0], conv_b[0],
                 dt_bias[0], a_log[0], d_skip[0], rel_bias, attn_out_norm[0], ssm_out_norm[0],
                 w_out[0], ffn_norm[0], peer_wq[0], peer_subkeys[0], peer_u[0], peer_v[0],
                 final_norm, buckets, far_bucket)
    return out.reshape(bsz, seq, d)
```

```python
import functools
import math

import numpy as np
import jax
import jax.numpy as jnp
from jax import lax
from jax.experimental import pallas as pl
from jax.experimental.pallas import tpu as pltpu

F32 = jnp.float32
BF16 = jnp.bfloat16
EPS = 1e-6
NEG = -1e30

LANES = 128
SUBLANES = 8
VMEM_LIMIT = 56 * 1024 * 1024

ATT_HEADS = 8
ATT_HEAD_DIM = 128
ATT_WIDTH = ATT_HEADS * ATT_HEAD_DIM
MOBA_BLOCK = 256
MOBA_TOPK = 3
MOBA_HEADS_PER_STEP = 4
REL_BUCKETS = 32
REL_MAX_DIST = 128
SSM_HEADS = 16
SSM_HEAD_DIM = 64
SSM_WIDTH = SSM_HEADS * SSM_HEAD_DIM
SSM_GROUPS = 2
SSM_STATE = 128
SSM_CONV = 4
SSD_CHUNK = 128
PEER_HEADS = 8
PEER_NKEYS = 128
PEER_HALF = 128
PEER_TOPK = 16
PEER_HK = PEER_HEADS * PEER_TOPK
PEER_TOKENS_PER_STEP = 16
PEER_RING_TOKENS = 8
PEER_ROW_PITCH = 20

HIGHEST = lax.Precision.HIGHEST


def _cparams(*sem):
    return pltpu.CompilerParams(dimension_semantics=sem, vmem_limit_bytes=VMEM_LIMIT)


def _rms(x, gain):
    return x * lax.rsqrt(jnp.mean(x * x, axis=-1, keepdims=True) + EPS) * gain


def _dot_nt(a, b, precision=None):
    return lax.dot_general(a, b, (((1,), (1,)), ((), ())), precision=precision,
                           preferred_element_type=F32)


def _norm_matmul_kernel(x_ref, g_ref, w_ref, o_ref, hn_ref):
    @pl.when(pl.program_id(1) == 0)
    def _():
        hn_ref[...] = _rms(x_ref[...], g_ref[...]).astype(BF16)

    o_ref[...] = jnp.dot(hn_ref[...], w_ref[...], preferred_element_type=F32)


def _norm_matmul(x, gain, w, tm, tn):
    t, d = x.shape
    n = w.shape[1]
    return pl.pallas_call(
        _norm_matmul_kernel,
        out_shape=jax.ShapeDtypeStruct((t, n), F32),
        grid=(t // tm, n // tn),
        in_specs=[pl.BlockSpec((tm, d), lambda i, j: (i, 0)),
                  pl.BlockSpec((1, d), lambda i, j: (0, 0)),
                  pl.BlockSpec((d, tn), lambda i, j: (0, j))],
        out_specs=pl.BlockSpec((tm, tn), lambda i, j: (i, j)),
        scratch_shapes=[pltpu.VMEM((tm, d), BF16)],
        compiler_params=_cparams("arbitrary", "arbitrary"),
        name="norm_matmul",
    )(x, gain.reshape(1, d), w)


def _in_proj_kernel(x_ref, g_ref, w_ref, wdt_ref, o_ref, dt_ref, dtt_ref, hn_ref):
    @pl.when(pl.program_id(1) == 0)
    def _():
        hn = _rms(x_ref[...], g_ref[...])
        hn_ref[...] = hn.astype(BF16)
        dt = jnp.dot(hn, wdt_ref[...], precision=HIGHEST, preferred_element_type=F32)
        dt_ref[...] = dt
        dtt_ref[...] = dt.T

    o_ref[...] = jnp.dot(hn_ref[...], w_ref[...], preferred_element_type=F32)


def _in_proj(x, gain, w_main, w_dt, tm, tn):
    t, d = x.shape
    n = w_main.shape[1]
    return pl.pallas_call(
        _in_proj_kernel,
        out_shape=(jax.ShapeDtypeStruct((t, n), F32),
                   jax.ShapeDtypeStruct((t, LANES), F32),
                   jax.ShapeDtypeStruct((LANES, t), F32)),
        grid=(t // tm, n // tn),
        in_specs=[pl.BlockSpec((tm, d), lambda i, j: (i, 0)),
                  pl.BlockSpec((1, d), lambda i, j: (0, 0)),
                  pl.BlockSpec((d, tn), lambda i, j: (0, j)),
                  pl.BlockSpec((d, LANES), lambda i, j: (0, 0))],
        out_specs=(pl.BlockSpec((tm, tn), lambda i, j: (i, j)),
                   pl.BlockSpec((tm, LANES), lambda i, j: (i, 0)),
                   pl.BlockSpec((LANES, tm), lambda i, j: (0, i))),
        scratch_shapes=[pltpu.VMEM((tm, d), BF16)],
        compiler_params=_cparams("arbitrary", "arbitrary"),
        name="in_proj",
    )(x, gain.reshape(1, d), w_main, w_dt)


def _t5_bucket_np(rel):
    n = np.maximum(rel, 0)
    max_exact = REL_BUCKETS // 2
    nf = np.maximum(n, 1).astype(np.float64)
    large = max_exact + (np.log(nf / max_exact) / math.log(REL_MAX_DIST / max_exact)
                         * (REL_BUCKETS - max_exact)).astype(np.int32)
    large = np.minimum(large, REL_BUCKETS - 1)
    return np.where(n < max_exact, n, large).astype(np.int32)


def _bucket_tiles(seq):
    ki = np.arange(MOBA_BLOCK)[:, None]
    qi = np.arange(MOBA_BLOCK)[None, :]
    own = _t5_bucket_np(qi - ki)
    prev = _t5_bucket_np(MOBA_BLOCK + qi - ki)
    far = _t5_bucket_np(np.arange(MOBA_BLOCK + 1, max(seq, MOBA_BLOCK + 2)))
    assert (far == far[0]).all(), "bias must be constant beyond the previous block"
    return np.stack([own, prev]), int(far[0])


def _bias_tiles_kernel(rb_ref, bk_ref, o_ref):
    h = pl.program_id(0)
    bk = bk_ref[...]
    acc = jnp.zeros(bk.shape, F32)
    for b in range(REL_BUCKETS):
        acc = jnp.where(bk == b, rb_ref[b, h], acc)
    o_ref[0] = acc


def _bias_tiles(rel_bias, buckets):
    nh = rel_bias.shape[1]
    return pl.pallas_call(
        _bias_tiles_kernel,
        out_shape=jax.ShapeDtypeStruct((nh, 2, MOBA_BLOCK, MOBA_BLOCK), F32),
        grid=(nh,),
        in_specs=[pl.BlockSpec(memory_space=pltpu.SMEM),
                  pl.BlockSpec((2, MOBA_BLOCK, MOBA_BLOCK), lambda h: (0, 0, 0))],
        out_specs=pl.BlockSpec((1, 2, MOBA_BLOCK, MOBA_BLOCK), lambda h: (h, 0, 0, 0)),
        compiler_params=_cparams("arbitrary"),
        name="bias_tiles",
    )(rel_bias, buckets)


def _moba_kernel(far_bucket, rb_ref, q_ref, k_ref, v_ref, bias_ref, o_ref, kmean_ref, kb_ref, vt_ref,
                 sown_ref, s_ref, pen_ref):
    hg = pl.program_id(1)
    c = pl.program_id(2)
    nh = kmean_ref.shape[0]
    nb = kmean_ref.shape[1]
    blk = MOBA_BLOCK
    hd = ATT_HEAD_DIM
    scale = hd ** -0.5

    @pl.when(c == 0)
    def _():
        k = k_ref[...]
        kb_ref[...] = k.astype(BF16)
        vt_ref[...] = jnp.transpose(v_ref[...]).astype(BF16)
        for hh in range(nh):
            kk = k[:, hh * hd:(hh + 1) * hd].reshape(nb, blk, hd)
            kmean_ref[hh] = jnp.sum(kk, axis=1) * (1.0 / blk)

    blk_id = lax.broadcasted_iota(jnp.int32, (nb, blk), 0)
    past = blk_id < c
    key_i = lax.broadcasted_iota(jnp.int32, (blk, blk), 0)
    qry_i = lax.broadcasted_iota(jnp.int32, (blk, blk), 1)
    own_off = pl.multiple_of(c * blk, blk)

    qbs, fars, init = [], [], []
    for hh in range(nh):
        hs = slice(hh * hd, (hh + 1) * hd)
        q = q_ref[:, hs]
        gate = _dot_nt(kmean_ref[hh], q, precision=HIGHEST)
        gm = jnp.where(past, gate, -jnp.inf)
        rank = jnp.zeros(gate.shape, jnp.int32)
        for j in range(nb):
            gj = gm[j:j + 1, :]
            beats = (gj > gm) | ((gj == gm) & (j < blk_id))
            rank = rank + beats.astype(jnp.int32)
        pen_ref[hh] = jnp.where(past & (rank < MOBA_TOPK), 0.0, NEG)
        qb = q.astype(BF16)
        qbs.append(qb)
        fars.append(rb_ref[far_bucket, hg * nh + hh])
        s = _dot_nt(kb_ref[pl.ds(own_off, blk), hs], qb) * scale + bias_ref[hh, 0]
        s = jnp.where(key_i <= qry_i, s, NEG)
        sown_ref[hh] = s
        init.append(jnp.max(s, axis=0, keepdims=True))

    npairs = (c + 1) // 2

    def scores(jp, ms):
        off = pl.multiple_of(jp * (2 * blk), 2 * blk)
        out = []
        for hh in range(nh):
            hs = slice(hh * hd, (hh + 1) * hd)
            sj = _dot_nt(kb_ref[pl.ds(off, 2 * blk), hs], qbs[hh]) * scale
            shift = jnp.concatenate(
                [jnp.where(2 * jp + u == c - 1, bias_ref[hh, 1], fars[hh]) + pen_ref[hh, pl.ds(2 * jp + u, 1), :]
                 for u in range(2)], axis=0)
            sj = sj + shift
            s_ref[hh, pl.ds(off, 2 * blk), :] = sj
            out.append(jnp.maximum(ms[hh], jnp.max(sj, axis=0, keepdims=True)))
        return tuple(out)

    ms = lax.fori_loop(0, npairs, scores, tuple(init))

    start = []
    for hh in range(nh):
        hs = slice(hh * hd, (hh + 1) * hd)
        p = jnp.exp(sown_ref[hh] - ms[hh])
        start.append((jnp.sum(p, axis=0, keepdims=True),
                      jnp.dot(vt_ref[hs, pl.ds(own_off, blk)], p.astype(BF16), preferred_element_type=F32)))

    def values(jp, carry):
        off = pl.multiple_of(jp * (2 * blk), 2 * blk)
        out = []
        for hh in range(nh):
            hs = slice(hh * hd, (hh + 1) * hd)
            l, acc = carry[hh]
            pj = jnp.exp(s_ref[hh, pl.ds(off, 2 * blk), :] - ms[hh])
            pv = jnp.dot(vt_ref[hs, pl.ds(off, 2 * blk)], pj.astype(BF16), preferred_element_type=F32)
            out.append((l + jnp.sum(pj, axis=0, keepdims=True), acc + pv))
        return tuple(out)

    final = lax.fori_loop(0, npairs, values, tuple(start))
    for hh in range(nh):
        l, acc = final[hh]
        o_ref[:, hh * hd:(hh + 1) * hd] = jnp.transpose(acc / l)


def _moba(proj, rel_bias, bias_tiles, far_bucket, bsz, seq):
    nb = seq // MOBA_BLOCK
    t = bsz * seq
    nh = MOBA_HEADS_PER_STEP
    gw = nh * ATT_HEAD_DIM
    ng = ATT_HEADS // nh
    return pl.pallas_call(
        functools.partial(_moba_kernel, far_bucket),
        out_shape=jax.ShapeDtypeStruct((t, ATT_WIDTH), F32),
        grid=(bsz, ng, nb),
        in_specs=[pl.BlockSpec(memory_space=pltpu.SMEM),
                  pl.BlockSpec((MOBA_BLOCK, gw), lambda b, g, c: (b * nb + c, g)),
                  pl.BlockSpec((seq, gw), lambda b, g, c: (b, ng + g)),
                  pl.BlockSpec((seq, gw), lambda b, g, c: (b, 2 * ng + g)),
                  pl.BlockSpec((nh, 2, MOBA_BLOCK, MOBA_BLOCK), lambda b, g, c: (g, 0, 0, 0))],
        out_specs=pl.BlockSpec((MOBA_BLOCK, gw), lambda b, g, c: (b * nb + c, g)),
        scratch_shapes=[pltpu.VMEM((nh, nb, ATT_HEAD_DIM), F32),
                        pltpu.VMEM((seq, gw), BF16),
                        pltpu.VMEM((gw, seq), BF16),
                        pltpu.VMEM((nh, MOBA_BLOCK, MOBA_BLOCK), F32),
                        pltpu.VMEM((nh, seq, MOBA_BLOCK), F32),
                        pltpu.VMEM((nh, nb, MOBA_BLOCK), F32)],
        compiler_params=_cparams("arbitrary", "arbitrary", "arbitrary"),
        name="moba",
    )(rel_bias, proj, proj, proj, bias_tiles)


def _silu(x):
    return x / (1.0 + jnp.exp(-x))


def _softplus(x):
    return jnp.maximum(x, 0.0) + jnp.log(1.0 + jnp.exp(-jnp.abs(x)))


def _ssd_kernel(z_ref, xs_ref, bc_ref, dt_ref, dtt_ref, cwx_ref, cwbc_ref, cbx_ref, cbbc_ref,
                dtb_ref, dtbt_ref, a_ref, at_ref, dsk_ref, gain_ref, o_ref,
                extx_ref, extbc_ref, st_ref, y_ref):
    c = pl.program_id(1)
    L = SSD_CHUNK
    P = SSM_HEAD_DIM
    N = SSM_STATE
    hpg = SSM_HEADS // SSM_GROUPS

    @pl.when(c == 0)
    def _():
        extx_ref[0:SUBLANES, :] = jnp.zeros((SUBLANES, SSM_WIDTH), F32)
        extbc_ref[0:SUBLANES, :] = jnp.zeros((SUBLANES, 2 * SSM_GROUPS * N), F32)
        st_ref[...] = jnp.zeros(st_ref.shape, F32)

    extx_ref[SUBLANES:SUBLANES + L, :] = xs_ref[...]
    extbc_ref[SUBLANES:SUBLANES + L, :] = bc_ref[...]

    def conv(ext_ref, w_ref, b_ref):
        acc = b_ref[...]
        for i in range(SSM_CONV):
            acc = acc + ext_ref[SUBLANES - i:SUBLANES - i + L, :] * w_ref[SSM_CONV - 1 - i:SSM_CONV - i, :]
        return _silu(acc)

    xs = conv(extx_ref, cwx_ref, cbx_ref)
    bcm = conv(extbc_ref, cwbc_ref, cbbc_ref)
    extx_ref[0:SUBLANES, :] = extx_ref[L:L + SUBLANES, :]
    extbc_ref[0:SUBLANES, :] = extbc_ref[L:L + SUBLANES, :]

    dt = _softplus(dt_ref[...] + dtb_ref[...])
    dtt = _softplus(dtt_ref[...] + dtbt_ref[...])
    da = dt * (-jnp.exp(a_ref[...]))
    dat = dtt * (-jnp.exp(at_ref[...]))
    r = lax.broadcasted_iota(jnp.int32, (L, L), 0)
    s_ = lax.broadcasted_iota(jnp.int32, (L, L), 1)
    causal = r >= s_
    tril = jnp.where(causal, 1.0, 0.0)
    triu = jnp.where(r <= s_, 1.0, 0.0)
    cs = jnp.dot(tril, da, precision=HIGHEST, preferred_element_type=F32)
    cst = jnp.dot(dat, triu, precision=HIGHEST, preferred_element_type=F32)
    ecs = jnp.exp(cs)
    cs_last = cs[L - 1:L, :]
    dst = jnp.exp(cs_last - cs)
    cdec = jnp.exp(cs_last)

    for g in range(SSM_GROUPS):
        bg = bcm[:, g * N:(g + 1) * N]
        cg = bcm[:, (SSM_GROUPS + g) * N:(SSM_GROUPS + g + 1) * N]
        bgb = bg.astype(BF16)
        cgb = cg.astype(BF16)
        cb = _dot_nt(cgb, bgb)
        bgt = jnp.transpose(bg).astype(BF16)
        for hh in range(hpg):
            h = g * hpg + hh
            xh = xs[:, h * P:(h + 1) * P]
            xdt = xh * dt[:, h:h + 1]
            seg = cs[:, h:h + 1] - cst[h:h + 1, :]
            decay = jnp.exp(jnp.where(causal, seg, NEG))
            y = jnp.dot((cb * decay).astype(BF16), xdt.astype(BF16), preferred_element_type=F32)
            st = st_ref[h]
            y = y + jnp.dot(cgb, st.astype(BF16), preferred_element_type=F32) * ecs[:, h:h + 1]
            y = y + xh * dsk_ref[:, h:h + 1]
            y_ref[:, h * P:(h + 1) * P] = y
            new = jnp.dot(bgt, (xdt * dst[:, h:h + 1]).astype(BF16), preferred_element_type=F32)
            st_ref[h] = st * cdec[:, h:h + 1] + new

    y = y_ref[...] * _silu(z_ref[...])
    gw = SSM_WIDTH // SSM_GROUPS
    for g in range(SSM_GROUPS):
        yg = y[:, g * gw:(g + 1) * gw]
        o_ref[:, g * gw:(g + 1) * gw] = _rms(yg, gain_ref[:, g * gw:(g + 1) * gw]).astype(o_ref.dtype)


def _pad_lanes(v):
    return jnp.zeros((1, LANES), F32).at[0, :v.shape[0]].set(v.astype(F32))


def _ssd(proj, dt, dtt, conv_w, conv_b, dt_bias, a_log, d_skip, gain, bsz, seq):
    t = bsz * seq
    nc = seq // SSD_CHUNK
    L = SSD_CHUNK
    nbc = 2 * SSM_GROUPS * SSM_STATE
    zcol = 3 * ATT_WIDTH // SSM_WIDTH
    xcol = zcol + 1
    bccol = (3 * ATT_WIDTH + 2 * SSM_WIDTH) // nbc
    dtb = _pad_lanes(dt_bias)
    alog = jnp.full((1, LANES), -jnp.inf, F32).at[0, :SSM_HEADS].set(a_log.astype(F32))
    dsk = _pad_lanes(d_skip)
    full = lambda shape: pl.BlockSpec(shape, lambda b, c: tuple(0 for _ in shape))
    return pl.pallas_call(
        _ssd_kernel,
        out_shape=jax.ShapeDtypeStruct((t, SSM_WIDTH), BF16),
        grid=(bsz, nc),
        in_specs=[pl.BlockSpec((L, SSM_WIDTH), lambda b, c: (b * nc + c, zcol)),
                  pl.BlockSpec((L, SSM_WIDTH), lambda b, c: (b * nc + c, xcol)),
                  pl.BlockSpec((L, nbc), lambda b, c: (b * nc + c, bccol)),
                  pl.BlockSpec((L, LANES), lambda b, c: (b * nc + c, 0)),
                  pl.BlockSpec((LANES, L), lambda b, c: (0, b * nc + c)),
                  full((SSM_CONV, SSM_WIDTH)), full((SSM_CONV, nbc)),
                  full((1, SSM_WIDTH)), full((1, nbc)),
                  full((1, LANES)), full((LANES, 1)), full((1, LANES)), full((LANES, 1)),
                  full((1, LANES)), full((1, SSM_WIDTH))],
        out_specs=pl.BlockSpec((L, SSM_WIDTH), lambda b, c: (b * nc + c, 0)),
        scratch_shapes=[pltpu.VMEM((L + 2 * SUBLANES, SSM_WIDTH), F32),
                        pltpu.VMEM((L + 2 * SUBLANES, nbc), F32),
                        pltpu.VMEM((SSM_HEADS, SSM_STATE, SSM_HEAD_DIM), F32),
                        pltpu.VMEM((L, SSM_WIDTH), F32)],
        compiler_params=_cparams("arbitrary", "arbitrary"),
        name="ssd",
    )(proj, proj, proj, dt, dtt,
      conv_w[:, :SSM_WIDTH], conv_w[:, SSM_WIDTH:],
      conv_b[:SSM_WIDTH].reshape(1, -1), conv_b[SSM_WIDTH:].reshape(1, -1),
      dtb, dtb.reshape(LANES, 1), alog, alog.reshape(LANES, 1), dsk, gain.reshape(1, -1))


def _out_proj_kernel(att_ref, y_ref, g_ref, x_ref, w_ref, o_ref, mix_ref):
    @pl.when(pl.program_id(1) == 0)
    def _():
        mix_ref[:, :ATT_WIDTH] = _rms(att_ref[...], g_ref[...]).astype(BF16)
        mix_ref[:, ATT_WIDTH:] = y_ref[...]

    o_ref[...] = x_ref[...] + jnp.dot(mix_ref[...], w_ref[...], preferred_element_type=F32)


def _out_proj(att, y, gain, x, w, tm, tn):
    t, d = x.shape
    k = w.shape[0]
    return pl.pallas_call(
        _out_proj_kernel,
        out_shape=jax.ShapeDtypeStruct((t, d), F32),
        grid=(t // tm, d // tn),
        in_specs=[pl.BlockSpec((tm, ATT_WIDTH), lambda i, j: (i, 0)),
                  pl.BlockSpec((tm, SSM_WIDTH), lambda i, j: (i, 0)),
                  pl.BlockSpec((1, ATT_WIDTH), lambda i, j: (0, 0)),
                  pl.BlockSpec((tm, tn), lambda i, j: (i, j)),
                  pl.BlockSpec((k, tn), lambda i, j: (0, j))],
        out_specs=pl.BlockSpec((tm, tn), lambda i, j: (i, j)),
        scratch_shapes=[pltpu.VMEM((tm, k), BF16)],
        compiler_params=_cparams("arbitrary", "arbitrary"),
        name="out_proj",
    )(att, y, gain.reshape(1, -1), x, w)


def _peer_topk_kernel(q_ref, sk_ref, e_ref, g_ref, v_ref, i_ref, top_ref, etop_ref):
    tb = q_ref.shape[0]
    K = PEER_TOPK
    S = SUBLANES
    key_iota = lax.broadcasted_iota(jnp.int32, (PEER_NKEYS, tb), 0).astype(F32)
    sub = lax.broadcasted_iota(jnp.int32, (S, tb), 0).astype(F32)
    for h in range(PEER_HEADS):
        for c in range(2):
            col = (2 * h + c) * PEER_HALF
            sc = _dot_nt(sk_ref[h, c], q_ref[:, col:col + PEER_HALF], precision=HIGHEST)
            for i in range(K):
                m = jnp.max(sc, axis=0, keepdims=True)
                idx = jnp.min(jnp.where(sc == m, key_iota, float(PEER_NKEYS)), axis=0, keepdims=True)
                v_ref[c, i:i + 1, :] = m
                i_ref[c, i:i + 1, :] = idx
                sc = jnp.where(key_iota == idx, -jnp.inf, sc)
        v2lo, v2hi = v_ref[1, 0:S, :], v_ref[1, S:K, :]
        i2lo, i2hi = i_ref[1, 0:S, :], i_ref[1, S:K, :]
        cands, eids, flats = [], [], []
        for a in range(S):
            cands.append(v_ref[0, a:a + 1, :] + v2lo)
            eids.append(i_ref[0, a:a + 1, :] * PEER_NKEYS + i2lo)
            flats.append(a * K + sub)
        cands.append(v_ref[0, 0:1, :] + v2hi)
        eids.append(i_ref[0, 0:1, :] * PEER_NKEYS + i2hi)
        flats.append(S + sub)
        cands.append(v_ref[0, S:K, :] + v_ref[1, 0:1, :])
        eids.append(i_ref[0, S:K, :] * PEER_NKEYS + i_ref[1, 0:1, :])
        flats.append((S + sub) * K)
        for i in range(K):
            m = functools.reduce(jnp.maximum, cands)
            m = jnp.max(m, axis=0, keepdims=True)
            pos = functools.reduce(jnp.minimum,
                                   [jnp.where(cd == m, fl, float(K * K)) for cd, fl in zip(cands, flats)])
            pos = jnp.min(pos, axis=0, keepdims=True)
            hits = [fl == pos for fl in flats]
            e = functools.reduce(jnp.add, [jnp.where(ht, ei, 0.0) for ht, ei in zip(hits, eids)])
            etop_ref[i:i + 1, :] = jnp.sum(e, axis=0, keepdims=True)
            top_ref[i:i + 1, :] = m
            cands = [jnp.where(ht, -jnp.inf, cd) for ht, cd in zip(hits, cands)]
        top_s = top_ref[...]
        p = jnp.exp(top_s - top_s[0:1, :])
        g_ref[h * K:(h + 1) * K, :] = p / jnp.sum(p, axis=0, keepdims=True)
        e_ref[h * K:(h + 1) * K, :] = etop_ref[...].astype(jnp.int32)


def _peer_topk(q, subkeys, tb):
    t, d = q.shape
    return pl.pallas_call(
        _peer_topk_kernel,
        out_shape=(jax.ShapeDtypeStruct((PEER_HK, t), jnp.int32),
                   jax.ShapeDtypeStruct((PEER_HK, t), F32)),
        grid=(t // tb,),
        in_specs=[pl.BlockSpec((tb, d), lambda i: (i, 0)),
                  pl.BlockSpec(subkeys.shape, lambda i: (0, 0, 0, 0))],
        out_specs=(pl.BlockSpec((PEER_HK, tb), lambda i: (0, i)),
                   pl.BlockSpec((PEER_HK, tb), lambda i: (0, i))),
        scratch_shapes=[pltpu.VMEM((2, PEER_TOPK, tb), F32),
                        pltpu.VMEM((2, PEER_TOPK, tb), F32),
                        pltpu.VMEM((PEER_TOPK, tb), F32),
                        pltpu.VMEM((PEER_TOPK, tb), F32)],
        compiler_params=_cparams("arbitrary"),
        name="peer_topk",
    )(q, subkeys)


def _gelu(x):
    return 0.5 * x * (1.0 + lax.erf(x * (1.0 / math.sqrt(2.0))))


def _unpack_bf16_pair(w):
    lo = lax.bitcast_convert_type(w << 16, F32)
    hi = lax.bitcast_convert_type(w & jnp.uint32(0xFFFF0000), F32)
    return lo, hi


def _peer_experts_kernel(x_ref, gates_ref, fg_ref, og_ref, ids_hbm, uv_hbm, o_ref,
                         buf0, buf1, buf2, ids_smem, sem, ids_sem):
    i = pl.program_id(0)
    n = pl.num_programs(0)
    sub = PEER_RING_TOKENS
    nsub = x_ref.shape[0] // sub
    nw = x_ref.shape[1] // (2 * LANES)
    rows = sub * PEER_HK
    pitch = PEER_ROW_PITCH
    bufs = (buf0, buf1, buf2)
    nbuf = len(bufs)
    assert nsub == 2 and nbuf == 3, "the ring arithmetic below is written for 2 sub-blocks, 3 buffers"

    def ids_copy(step, slot):
        return pltpu.make_async_copy(ids_hbm.at[step], ids_smem.at[slot], ids_sem.at[slot])

    def issue_token(slot, u, j, dst, dst_sem):
        for k in range(PEER_HK):
            r = j * PEER_HK + k
            e = ids_smem[slot, 0, u * rows + r]
            src = uv_hbm.at[pl.ds(pl.multiple_of(e * (2 * nw), 2 * nw), 2 * nw), :]
            pltpu.make_async_copy(src, dst.at[pl.ds(r * pitch, 2 * nw), :], dst_sem).start(priority=k % 2)

    def wait_all(b):
        window = pl.ds(0, rows * 2 * nw)
        pltpu.make_async_copy(bufs[(b + 1) % nbuf].at[window, :], bufs[b].at[window, :], sem.at[b]).wait()

    def chunk(buf, j, s):
        return buf[pl.ds(j * PEER_HK * pitch + s, PEER_HK, stride=pitch), :]

    @pl.when(i == 0)
    def _():
        ids_copy(0, 0).start()
        ids_copy(jnp.minimum(1, n - 1), 2).start()
        ids_copy(0, 0).wait()
        for u in range(nsub):
            for j in range(sub):
                issue_token(0, u, j, bufs[u], sem.at[u])

    def step(p):
        nxt_slot = (p + 2) % nbuf
        ids_copy(0, nxt_slot).wait()
        ids_copy(jnp.minimum(i + 2, n - 1), (p + 1) % nbuf).start()
        x = x_ref[...]
        hx = _rms(x, fg_ref[...])
        gates = gates_ref[...]
        eye = (lax.broadcasted_iota(jnp.int32, (PEER_HK, PEER_HK), 0)
               == lax.broadcasted_iota(jnp.int32, (PEER_HK, PEER_HK), 1))
        outs = []
        for u in range(nsub):
            b = (p + u) % nbuf
            ahead = (b + 2) % nbuf
            cur = bufs[b]
            wait_all(b)
            for j in range(sub):
                t = u * sub + j
                acc = jnp.zeros((PEER_HK, LANES), F32)
                for s in range(nw):
                    lo, hi = _unpack_bf16_pair(chunk(cur, j, s))
                    acc = acc + lo * hx[t:t + 1, s * LANES:(s + 1) * LANES]
                    acc = acc + hi * hx[t:t + 1, (nw + s) * LANES:(nw + s + 1) * LANES]
                pre = jnp.sum(acc, axis=1, keepdims=True)
                gcol = jnp.sum(jnp.where(eye, gates[t:t + 1, :], 0.0), axis=1, keepdims=True)
                act = _gelu(pre) * gcol
                los, his = [], []
                for s in range(nw):
                    lo, hi = _unpack_bf16_pair(chunk(cur, j, nw + s))
                    los.append(jnp.sum(lo * act, axis=0, keepdims=True))
                    his.append(jnp.sum(hi * act, axis=0, keepdims=True))
                outs.append(jnp.concatenate(los + his, axis=1))
                issue_token(nxt_slot, u, j, bufs[ahead], sem.at[ahead])
        x2 = x + jnp.concatenate(outs, axis=0)
        o_ref[...] = _rms(x2, og_ref[...])

        @pl.when(i == n - 1)
        def _():
            wait_all((p + 2) % nbuf)
            wait_all(p)
            ids_copy(0, (p + 1) % nbuf).wait()

    for p in range(nbuf):
        pl.when(lax.rem(nsub * i, nbuf) == p)(functools.partial(step, p))


def _peer_experts(x1, eidx, gates, ffn_gain, out_gain, uv_tab):
    t, d = x1.shape
    tps = PEER_TOKENS_PER_STEP
    n = t // tps
    rows = PEER_RING_TOKENS * PEER_HK
    buf = pltpu.VMEM((rows * PEER_ROW_PITCH, LANES), jnp.uint32)
    return pl.pallas_call(
        _peer_experts_kernel,
        out_shape=jax.ShapeDtypeStruct((t, d), F32),
        grid=(n,),
        in_specs=[pl.BlockSpec((tps, d), lambda i: (i, 0)),
                  pl.BlockSpec((tps, PEER_HK), lambda i: (i, 0)),
                  pl.BlockSpec((1, d), lambda i: (0, 0)),
                  pl.BlockSpec((1, d), lambda i: (0, 0)),
                  pl.BlockSpec(memory_space=pl.ANY),
                  pl.BlockSpec(memory_space=pl.ANY)],
        out_specs=pl.BlockSpec((tps, d), lambda i: (i, 0)),
        scratch_shapes=[buf, buf, buf,
                        pltpu.SMEM((3, 1, tps * PEER_HK), jnp.int32),
                        pltpu.SemaphoreType.DMA((3,)),
                        pltpu.SemaphoreType.DMA((3,))],
        compiler_params=_cparams("arbitrary"),
        name="peer_experts",
    )(x1, gates, ffn_gain.reshape(1, d), out_gain.reshape(1, d), eidx.reshape(n, 1, tps * PEER_HK), uv_tab)


def _pack_tables_kernel(u_ref, v_ref, o_ref):
    eb, d = u_ref.shape
    nw = d // (2 * LANES)

    def bf16_bits(x):
        return lax.bitcast_convert_type(x.astype(BF16).astype(F32), jnp.uint32)

    for t, ref in enumerate((u_ref, v_ref)):
        for s in range(nw):
            lo = bf16_bits(ref[:, s * LANES:(s + 1) * LANES]) >> 16
            hi = bf16_bits(ref[:, (nw + s) * LANES:(nw + s + 1) * LANES])
            o_ref[pl.ds(t * nw + s, eb, stride=2 * nw), :] = hi | lo


def _merge_expert_tables(u_tab, v_tab):
    ne, d = u_tab.shape
    nw = d // (2 * LANES)
    eb = 256
    return pl.pallas_call(
        _pack_tables_kernel,
        out_shape=jax.ShapeDtypeStruct((ne * 2 * nw, LANES), jnp.uint32),
        grid=(ne // eb,),
        in_specs=[pl.BlockSpec((eb, d), lambda i: (i, 0)),
                  pl.BlockSpec((eb, d), lambda i: (i, 0))],
        out_specs=pl.BlockSpec((eb * 2 * nw, LANES), lambda i: (i, 0)),
        compiler_params=_cparams("arbitrary"),
        name="pack_tables",
    )(u_tab, v_tab)


def _layer(x2d, bsz, seq, attn_norm, w_in, conv_w, conv_b, dt_bias, a_log, d_skip, rel_bias,
           attn_out_norm, ssm_out_norm, w_out, ffn_norm, peer_wq, peer_subkeys, peer_u, peer_v,
           out_gain, buckets, far_bucket):
    n_main = 3 * ATT_WIDTH + 2 * SSM_WIDTH + 2 * SSM_GROUPS * SSM_STATE
    w_main = w_in[:, :n_main].astype(BF16)
    w_dt = jnp.zeros((w_in.shape[0], LANES), F32).at[:, :SSM_HEADS].set(w_in[:, n_main:])
    proj, dt, dtt = _in_proj(x2d, attn_norm, w_main, w_dt, 1024, 512)
    tiles = _bias_tiles(rel_bias, buckets)
    att = _moba(proj, rel_bias, tiles, far_bucket, bsz, seq)
    y = _ssd(proj, dt, dtt, conv_w, conv_b, dt_bias, a_log, d_skip, ssm_out_norm, bsz, seq)
    d_model = x2d.shape[1]
    x1 = _out_proj(att, y, attn_out_norm, x2d, w_out.astype(BF16), 512, d_model)
    pq = _norm_matmul(x1, ffn_norm, peer_wq.astype(BF16), 512, d_model)
    e_t, g_t = _peer_topk(pq, peer_subkeys, LANES)
    uv = _merge_expert_tables(peer_u, peer_v)
    return _peer_experts(x1, e_t.T, g_t.T, ffn_norm, out_gain, uv)


def kernel(x, attn_norm, w_in, conv_w, conv_b, dt_bias, a_log, d_skip, rel_bias, attn_out_norm,
           ssm_out_norm, w_out, ffn_norm, peer_wq, peer_subkeys, peer_u, peer_v, final_norm):
    bsz, seq, d = x.shape
    depth = w_in.shape[0]
    assert depth == 1, "the final rmsnorm is fused into the last layer's expert kernel"
    buckets_np, far_bucket = _bucket_tiles(seq)
    buckets = jnp.asarray(buckets_np)
    out = _layer(x.reshape(bsz * seq, d), bsz, seq, attn_norm[0], w_in[0], conv_w[0], conv_b[0],
                 dt_bias[0], a_log[0], d_skip[0], rel_bias, attn_out_norm[0], ssm_out_norm[0],
                 w_out[0], ffn_norm[0], peer_wq[0], peer_subkeys[0], peer_u[0], peer_v[0],
                 final_norm, buckets, far_bucket)
    return out.reshape(bsz, seq, d)
```

```python
import functools
import math

import numpy as np
import jax
import jax.numpy as jnp
from jax import lax
from jax.experimental import pallas as pl
from jax.experimental.pallas import tpu as pltpu

F32 = jnp.float32
BF16 = jnp.bfloat16
EPS = 1e-6
NEG = -1e30

LANES = 128
SUBLANES = 8
VMEM_LIMIT = 56 * 1024 * 1024

ATT_HEADS = 8
ATT_HEAD_DIM = 128
ATT_WIDTH = ATT_HEADS * ATT_HEAD_DIM
MOBA_BLOCK = 256
MOBA_TOPK = 3
MOBA_HEADS_PER_STEP = 4
REL_BUCKETS = 32
REL_MAX_DIST = 128
SSM_HEADS = 16
SSM_HEAD_DIM = 64
SSM_WIDTH = SSM_HEADS * SSM_HEAD_DIM
SSM_GROUPS = 2
SSM_STATE = 128
SSM_CONV = 4
SSD_CHUNK = 128
PEER_HEADS = 8
PEER_NKEYS = 128
PEER_HALF = 128
PEER_TOPK = 16
PEER_HK = PEER_HEADS * PEER_TOPK
PEER_TOKENS_PER_STEP = 16
PEER_RING_TOKENS = 8
PEER_ROW_PITCH = 20

HIGHEST = lax.Precision.HIGHEST


def _cparams(*sem):
    return pltpu.CompilerParams(dimension_semantics=sem, vmem_limit_bytes=VMEM_LIMIT)


def _rms(x, gain):
    return x * lax.rsqrt(jnp.mean(x * x, axis=-1, keepdims=True) + EPS) * gain


def _dot_nt(a, b, precision=None):
    return lax.dot_general(a, b, (((1,), (1,)), ((), ())), precision=precision,
                           preferred_element_type=F32)


def _norm_matmul_kernel(x_ref, g_ref, w_ref, o_ref, hn_ref):
    @pl.when(pl.program_id(1) == 0)
    def _():
        hn_ref[...] = _rms(x_ref[...], g_ref[...]).astype(BF16)

    o_ref[...] = jnp.dot(hn_ref[...], w_ref[...], preferred_element_type=F32)


def _norm_matmul(x, gain, w, tm, tn):
    t, d = x.shape
    n = w.shape[1]
    return pl.pallas_call(
        _norm_matmul_kernel,
        out_shape=jax.ShapeDtypeStruct((t, n), F32),
        grid=(t // tm, n // tn),
        in_specs=[pl.BlockSpec((tm, d), lambda i, j: (i, 0)),
                  pl.BlockSpec((1, d), lambda i, j: (0, 0)),
                  pl.BlockSpec((d, tn), lambda i, j: (0, j))],
        out_specs=pl.BlockSpec((tm, tn), lambda i, j: (i, j)),
        scratch_shapes=[pltpu.VMEM((tm, d), BF16)],
        compiler_params=_cparams("arbitrary", "arbitrary"),
        name="norm_matmul",
    )(x, gain.reshape(1, d), w)


def _in_proj_kernel(x_ref, g_ref, w_ref, wdt_ref, o_ref, dt_ref, dtt_ref, hn_ref):
    @pl.when(pl.program_id(1) == 0)
    def _():
        hn = _rms(x_ref[...], g_ref[...])
        hn_ref[...] = hn.astype(BF16)
        dt = jnp.dot(hn, wdt_ref[...], precision=HIGHEST, preferred_element_type=F32)
        dt_ref[...] = dt
        dtt_ref[...] = dt.T

    o_ref[...] = jnp.dot(hn_ref[...], w_ref[...], preferred_element_type=F32)


def _in_proj(x, gain, w_main, w_dt, tm, tn):
    t, d = x.shape
    n = w_main.shape[1]
    return pl.pallas_call(
        _in_proj_kernel,
        out_shape=(jax.ShapeDtypeStruct((t, n), F32),
                   jax.ShapeDtypeStruct((t, LANES), F32),
                   jax.ShapeDtypeStruct((LANES, t), F32)),
        grid=(t // tm, n // tn),
        in_specs=[pl.BlockSpec((tm, d), lambda i, j: (i, 0)),
                  pl.BlockSpec((1, d), lambda i, j: (0, 0)),
                  pl.BlockSpec((d, tn), lambda i, j: (0, j)),
                  pl.BlockSpec((d, LANES), lambda i, j: (0, 0))],
        out_specs=(pl.BlockSpec((tm, tn), lambda i, j: (i, j)),
                   pl.BlockSpec((tm, LANES), lambda i, j: (i, 0)),
                   pl.BlockSpec((LANES, tm), lambda i, j: (0, i))),
        scratch_shapes=[pltpu.VMEM((tm, d), BF16)],
        compiler_params=_cparams("arbitrary", "arbitrary"),
        name="in_proj",
    )(x, gain.reshape(1, d), w_main, w_dt)


def _t5_bucket_np(rel):
    n = np.maximum(rel, 0)
    max_exact = REL_BUCKETS // 2
    nf = np.maximum(n, 1).astype(np.float64)
    large = max_exact + (np.log(nf / max_exact) / math.log(REL_MAX_DIST / max_exact)
                         * (REL_BUCKETS - max_exact)).astype(np.int32)
    large = np.minimum(large, REL_BUCKETS - 1)
    return np.where(n < max_exact, n, large).astype(np.int32)


def _bucket_tiles(seq):
    ki = np.arange(MOBA_BLOCK)[:, None]
    qi = np.arange(MOBA_BLOCK)[None, :]
    own = _t5_bucket_np(qi - ki)
    prev = _t5_bucket_np(MOBA_BLOCK + qi - ki)
    far = _t5_bucket_np(np.arange(MOBA_BLOCK + 1, max(seq, MOBA_BLOCK + 2)))
    assert (far == far[0]).all(), "bias must be constant beyond the previous block"
    return np.stack([own, prev]), int(far[0])


def _bias_tiles_kernel(rb_ref, bk_ref, o_ref):
    h = pl.program_id(0)
    bk = bk_ref[...]
    acc = jnp.zeros(bk.shape, F32)
    for b in range(REL_BUCKETS):
        acc = jnp.where(bk == b, rb_ref[b, h], acc)
    o_ref[0] = acc


def _bias_tiles(rel_bias, buckets):
    nh = rel_bias.shape[1]
    return pl.pallas_call(
        _bias_tiles_kernel,
        out_shape=jax.ShapeDtypeStruct((nh, 2, MOBA_BLOCK, MOBA_BLOCK), F32),
        grid=(nh,),
        in_specs=[pl.BlockSpec(memory_space=pltpu.SMEM),
                  pl.BlockSpec((2, MOBA_BLOCK, MOBA_BLOCK), lambda h: (0, 0, 0))],
        out_specs=pl.BlockSpec((1, 2, MOBA_BLOCK, MOBA_BLOCK), lambda h: (h, 0, 0, 0)),
        compiler_params=_cparams("arbitrary"),
        name="bias_tiles",
    )(rel_bias, buckets)


def _moba_kernel(far_bucket, rb_ref, q_ref, k_ref, v_ref, bias_ref, o_ref, kmean_ref, kb_ref, vt_ref,
                 sown_ref, s_ref, pen_ref):
    hg = pl.program_id(1)
    c = pl.program_id(2)
    nh = kmean_ref.shape[0]
    nb = kmean_ref.shape[1]
    blk = MOBA_BLOCK
    hd = ATT_HEAD_DIM
    scale = hd ** -0.5

    @pl.when(c == 0)
    def _():
        k = k_ref[...]
        kb_ref[...] = k.astype(BF16)
        vt_ref[...] = jnp.transpose(v_ref[...]).astype(BF16)
        for hh in range(nh):
            kk = k[:, hh * hd:(hh + 1) * hd].reshape(nb, blk, hd)
            kmean_ref[hh] = jnp.sum(kk, axis=1) * (1.0 / blk)

    blk_id = lax.broadcasted_iota(jnp.int32, (nb, blk), 0)
    past = blk_id < c
    key_i = lax.broadcasted_iota(jnp.int32, (blk, blk), 0)
    qry_i = lax.broadcasted_iota(jnp.int32, (blk, blk), 1)
    own_off = pl.multiple_of(c * blk, blk)

    qbs, fars, init = [], [], []
    for hh in range(nh):
        hs = slice(hh * hd, (hh + 1) * hd)
        q = q_ref[:, hs]
        gate = _dot_nt(kmean_ref[hh], q, precision=HIGHEST)
        gm = jnp.where(past, gate, -jnp.inf)
        rank = jnp.zeros(gate.shape, jnp.int32)
        for j in range(nb):
            gj = gm[j:j + 1, :]
            beats = (gj > gm) | ((gj == gm) & (j < blk_id))
            rank = rank + beats.astype(jnp.int32)
        pen_ref[hh] = jnp.where(past & (rank < MOBA_TOPK), 0.0, NEG)
        qb = q.astype(BF16)
        qbs.append(qb)
        fars.append(rb_ref[far_bucket, hg * nh + hh])
        s = _dot_nt(kb_ref[pl.ds(own_off, blk), hs], qb) * scale + bias_ref[hh, 0]
        s = jnp.where(key_i <= qry_i, s, NEG)
        sown_ref[hh] = s
        init.append(jnp.max(s, axis=0, keepdims=True))

    npairs = (c + 1) // 2

    def scores(jp, ms):
        off = pl.multiple_of(jp * (2 * blk), 2 * blk)
        out = []
        for hh in range(nh):
            hs = slice(hh * hd, (hh + 1) * hd)
            sj = _dot_nt(kb_ref[pl.ds(off, 2 * blk), hs], qbs[hh]) * scale
            shift = jnp.concatenate(
                [jnp.where(2 * jp + u == c - 1, bias_ref[hh, 1], fars[hh]) + pen_ref[hh, pl.ds(2 * jp + u, 1), :]
                 for u in range(2)], axis=0)
            sj = sj + shift
            s_ref[hh, pl.ds(off, 2 * blk), :] = sj
            out.append(jnp.maximum(ms[hh], jnp.max(sj, axis=0, keepdims=True)))
        return tuple(out)

    ms = lax.fori_loop(0, npairs, scores, tuple(init))

    start = []
    for hh in range(nh):
        hs = slice(hh * hd, (hh + 1) * hd)
        p = jnp.exp(sown_ref[hh] - ms[hh])
        start.append((jnp.sum(p, axis=0, keepdims=True),
                      jnp.dot(vt_ref[hs, pl.ds(own_off, blk)], p.astype(BF16), preferred_element_type=F32)))

    def values(jp, carry):
        off = pl.multiple_of(jp * (2 * blk), 2 * blk)
        out = []
        for hh in range(nh):
            hs = slice(hh * hd, (hh + 1) * hd)
            l, acc = carry[hh]
            pj = jnp.exp(s_ref[hh, pl.ds(off, 2 * blk), :] - ms[hh])
            pv = jnp.dot(vt_ref[hs, pl.ds(off, 2 * blk)], pj.astype(BF16), preferred_element_type=F32)
            out.append((l + jnp.sum(pj, axis=0, keepdims=True), acc + pv))
        return tuple(out)

    final = lax.fori_loop(0, npairs, values, tuple(start))
    for hh in range(nh):
        l, acc = final[hh]
        o_ref[:, hh * hd:(hh + 1) * hd] = jnp.transpose(acc / l)


def _moba(proj, rel_bias, bias_tiles, far_bucket, bsz, seq):
    nb = seq // MOBA_BLOCK
    t = bsz * seq
    nh = MOBA_HEADS_PER_STEP
    gw = nh * ATT_HEAD_DIM
    ng = ATT_HEADS // nh
    return pl.pallas_call(
        functools.partial(_moba_kernel, far_bucket),
        out_shape=jax.ShapeDtypeStruct((t, ATT_WIDTH), F32),
        grid=(bsz, ng, nb),
        in_specs=[pl.BlockSpec(memory_space=pltpu.SMEM),
                  pl.BlockSpec((MOBA_BLOCK, gw), lambda b, g, c: (b * nb + c, g)),
                  pl.BlockSpec((seq, gw), lambda b, g, c: (b, ng + g)),
                  pl.BlockSpec((seq, gw), lambda b, g, c: (b, 2 * ng + g)),
                  pl.BlockSpec((nh, 2, MOBA_BLOCK, MOBA_BLOCK), lambda b, g, c: (g, 0, 0, 0))],
        out_specs=pl.BlockSpec((MOBA_BLOCK, gw), lambda b, g, c: (b * nb + c, g)),
        scratch_shapes=[pltpu.VMEM((nh, nb, ATT_HEAD_DIM), F32),
                        pltpu.VMEM((seq, gw), BF16),
                        pltpu.VMEM((gw, seq), BF16),
                        pltpu.VMEM((nh, MOBA_BLOCK, MOBA_BLOCK), F32),
                        pltpu.VMEM((nh, seq, MOBA_BLOCK), F32),
                        pltpu.VMEM((nh, nb, MOBA_BLOCK), F32)],
        compiler_params=_cparams("arbitrary", "arbitrary", "arbitrary"),
        name="moba",
    )(rel_bias, proj, proj, proj, bias_tiles)


def _silu(x):
    return x / (1.0 + jnp.exp(-x))


def _softplus(x):
    return jnp.maximum(x, 0.0) + jnp.log(1.0 + jnp.exp(-jnp.abs(x)))


def _ssd_kernel(z_ref, xs_ref, bc_ref, dt_ref, dtt_ref, cwx_ref, cwbc_ref, cbx_ref, cbbc_ref,
                dtb_ref, dtbt_ref, a_ref, at_ref, dskw_ref, gain_ref, exp_ref, o_ref,
                extx_ref, extbc_ref, st_ref, y_ref):
    c = pl.program_id(1)
    L = SSD_CHUNK
    P = SSM_HEAD_DIM
    N = SSM_STATE
    hpg = SSM_HEADS // SSM_GROUPS

    @pl.when(c == 0)
    def _():
        extx_ref[0:SUBLANES, :] = jnp.zeros((SUBLANES, SSM_WIDTH), F32)
        extbc_ref[0:SUBLANES, :] = jnp.zeros((SUBLANES, 2 * SSM_GROUPS * N), F32)
        st_ref[...] = jnp.zeros(st_ref.shape, F32)

    extx_ref[SUBLANES:SUBLANES + L, :] = xs_ref[...]
    extbc_ref[SUBLANES:SUBLANES + L, :] = bc_ref[...]

    def conv(ext_ref, w_ref, b_ref):
        acc = b_ref[...]
        for i in range(SSM_CONV):
            acc = acc + ext_ref[SUBLANES - i:SUBLANES - i + L, :] * w_ref[SSM_CONV - 1 - i:SSM_CONV - i, :]
        return _silu(acc)

    xs = conv(extx_ref, cwx_ref, cbx_ref)
    bcm = conv(extbc_ref, cwbc_ref, cbbc_ref)
    extx_ref[0:SUBLANES, :] = extx_ref[L:L + SUBLANES, :]
    extbc_ref[0:SUBLANES, :] = extbc_ref[L:L + SUBLANES, :]

    dt = _softplus(dt_ref[...] + dtb_ref[...])
    dtt = _softplus(dtt_ref[...] + dtbt_ref[...])
    da = dt * (-jnp.exp(a_ref[...]))
    dat = dtt * (-jnp.exp(at_ref[...]))
    r = lax.broadcasted_iota(jnp.int32, (L, L), 0)
    s_ = lax.broadcasted_iota(jnp.int32, (L, L), 1)
    causal = r >= s_
    tril = jnp.where(causal, 1.0, 0.0)
    triu = jnp.where(r <= s_, 1.0, 0.0)
    cs = jnp.dot(tril, da, precision=HIGHEST, preferred_element_type=F32)
    cst = jnp.dot(dat, triu, precision=HIGHEST, preferred_element_type=F32)
    ecs = jnp.exp(cs)
    cs_last = cs[L - 1:L, :]
    dst = jnp.exp(cs_last - cs)

    cols = jnp.concatenate([dt, ecs, dst], axis=0)
    expand = exp_ref[...]
    wide = jnp.zeros((3 * L, SSM_WIDTH), F32)
    rest = cols
    for _ in range(3):
        piece = rest.astype(BF16)
        wide = wide + jnp.dot(piece, expand, preferred_element_type=F32)
        rest = rest - piece.astype(F32)
    dt_w, ecs_w, dst_w = wide[0:L], wide[L:2 * L], wide[2 * L:3 * L]
    xdt_w = xs * dt_w
    xdt_b = xdt_w.astype(BF16)
    xend_b = (xdt_w * dst_w).astype(BF16)
    skip_w = xs * dskw_ref[...]

    for g in range(SSM_GROUPS):
        bg = bcm[:, g * N:(g + 1) * N]
        cg = bcm[:, (SSM_GROUPS + g) * N:(SSM_GROUPS + g + 1) * N]
        bgb = bg.astype(BF16)
        cgb = cg.astype(BF16)
        cb = _dot_nt(cgb, bgb)
        bgt = jnp.transpose(bg).astype(BF16)
        for hh in range(hpg):
            h = g * hpg + hh
            hp = slice(h * P, (h + 1) * P)
            seg = cs[:, h:h + 1] - cst[h:h + 1, :]
            decay = jnp.exp(jnp.where(causal, seg, NEG))
            y = jnp.dot((cb * decay).astype(BF16), xdt_b[:, hp], preferred_element_type=F32)
            st = st_ref[h]
            y = y + jnp.dot(cgb, st.astype(BF16), preferred_element_type=F32) * ecs_w[:, hp]
            y_ref[:, hp] = y + skip_w[:, hp]
            new = jnp.dot(bgt, xend_b[:, hp], preferred_element_type=F32)
            st_ref[h] = st * ecs_w[L - 1:L, hp] + new

    y = y_ref[...] * _silu(z_ref[...])
    gw = SSM_WIDTH // SSM_GROUPS
    for g in range(SSM_GROUPS):
        yg = y[:, g * gw:(g + 1) * gw]
        o_ref[:, g * gw:(g + 1) * gw] = _rms(yg, gain_ref[:, g * gw:(g + 1) * gw]).astype(o_ref.dtype)


def _pad_lanes(v):
    return jnp.zeros((1, LANES), F32).at[0, :v.shape[0]].set(v.astype(F32))


def _ssd(proj, dt, dtt, conv_w, conv_b, dt_bias, a_log, d_skip, gain, bsz, seq):
    t = bsz * seq
    nc = seq // SSD_CHUNK
    L = SSD_CHUNK
    nbc = 2 * SSM_GROUPS * SSM_STATE
    zcol = 3 * ATT_WIDTH // SSM_WIDTH
    xcol = zcol + 1
    bccol = (3 * ATT_WIDTH + 2 * SSM_WIDTH) // nbc
    dtb = _pad_lanes(dt_bias)
    alog = jnp.full((1, LANES), -jnp.inf, F32).at[0, :SSM_HEADS].set(a_log.astype(F32))
    dsk_wide = jnp.repeat(d_skip.astype(F32), SSM_HEAD_DIM).reshape(1, SSM_WIDTH)
    expand = jnp.asarray(np.arange(LANES)[:, None] == np.arange(SSM_WIDTH)[None, :] // SSM_HEAD_DIM, BF16)
    full = lambda shape: pl.BlockSpec(shape, lambda b, c: tuple(0 for _ in shape))
    return pl.pallas_call(
        _ssd_kernel,
        out_shape=jax.ShapeDtypeStruct((t, SSM_WIDTH), BF16),
        grid=(bsz, nc),
        in_specs=[pl.BlockSpec((L, SSM_WIDTH), lambda b, c: (b * nc + c, zcol)),
                  pl.BlockSpec((L, SSM_WIDTH), lambda b, c: (b * nc + c, xcol)),
                  pl.BlockSpec((L, nbc), lambda b, c: (b * nc + c, bccol)),
                  pl.BlockSpec((L, LANES), lambda b, c: (b * nc + c, 0)),
                  pl.BlockSpec((LANES, L), lambda b, c: (0, b * nc + c)),
                  full((SSM_CONV, SSM_WIDTH)), full((SSM_CONV, nbc)),
                  full((1, SSM_WIDTH)), full((1, nbc)),
                  full((1, LANES)), full((LANES, 1)), full((1, LANES)), full((LANES, 1)),
                  full((1, SSM_WIDTH)), full((1, SSM_WIDTH)), full((LANES, SSM_WIDTH))],
        out_specs=pl.BlockSpec((L, SSM_WIDTH), lambda b, c: (b * nc + c, 0)),
        scratch_shapes=[pltpu.VMEM((L + 2 * SUBLANES, SSM_WIDTH), F32),
                        pltpu.VMEM((L + 2 * SUBLANES, nbc), F32),
                        pltpu.VMEM((SSM_HEADS, SSM_STATE, SSM_HEAD_DIM), F32),
                        pltpu.VMEM((L, SSM_WIDTH), F32)],
        compiler_params=_cparams("arbitrary", "arbitrary"),
        name="ssd",
    )(proj, proj, proj, dt, dtt,
      conv_w[:, :SSM_WIDTH], conv_w[:, SSM_WIDTH:],
      conv_b[:SSM_WIDTH].reshape(1, -1), conv_b[SSM_WIDTH:].reshape(1, -1),
      dtb, dtb.reshape(LANES, 1), alog, alog.reshape(LANES, 1), dsk_wide, gain.reshape(1, -1), expand)


def _out_proj_kernel(att_ref, y_ref, g_ref, x_ref, w_ref, o_ref, mix_ref):
    @pl.when(pl.program_id(1) == 0)
    def _():
        mix_ref[:, :ATT_WIDTH] = _rms(att_ref[...], g_ref[...]).astype(BF16)
        mix_ref[:, ATT_WIDTH:] = y_ref[...]

    o_ref[...] = x_ref[...] + jnp.dot(mix_ref[...], w_ref[...], preferred_element_type=F32)


def _out_proj(att, y, gain, x, w, tm, tn):
    t, d = x.shape
    k = w.shape[0]
    return pl.pallas_call(
        _out_proj_kernel,
        out_shape=jax.ShapeDtypeStruct((t, d), F32),
        grid=(t // tm, d // tn),
        in_specs=[pl.BlockSpec((tm, ATT_WIDTH), lambda i, j: (i, 0)),
                  pl.BlockSpec((tm, SSM_WIDTH), lambda i, j: (i, 0)),
                  pl.BlockSpec((1, ATT_WIDTH), lambda i, j: (0, 0)),
                  pl.BlockSpec((tm, tn), lambda i, j: (i, j)),
                  pl.BlockSpec((k, tn), lambda i, j: (0, j))],
        out_specs=pl.BlockSpec((tm, tn), lambda i, j: (i, j)),
        scratch_shapes=[pltpu.VMEM((tm, k), BF16)],
        compiler_params=_cparams("arbitrary", "arbitrary"),
        name="out_proj",
    )(att, y, gain.reshape(1, -1), x, w)


def _peer_topk_kernel(q_ref, sk_ref, e_ref, g_ref, v_ref, i_ref, top_ref, etop_ref):
    tb = q_ref.shape[0]
    K = PEER_TOPK
    S = SUBLANES
    key_iota = lax.broadcasted_iota(jnp.int32, (PEER_NKEYS, tb), 0).astype(F32)
    sub = lax.broadcasted_iota(jnp.int32, (S, tb), 0).astype(F32)
    for h in range(PEER_HEADS):
        for c in range(2):
            col = (2 * h + c) * PEER_HALF
            sc = _dot_nt(sk_ref[h, c], q_ref[:, col:col + PEER_HALF], precision=HIGHEST)
            for i in range(K):
                m = jnp.max(sc, axis=0, keepdims=True)
                idx = jnp.min(jnp.where(sc == m, key_iota, float(PEER_NKEYS)), axis=0, keepdims=True)
                v_ref[c, i:i + 1, :] = m
                i_ref[c, i:i + 1, :] = idx
                sc = jnp.where(key_iota == idx, -jnp.inf, sc)
        v2lo, v2hi = v_ref[1, 0:S, :], v_ref[1, S:K, :]
        i2lo, i2hi = i_ref[1, 0:S, :], i_ref[1, S:K, :]
        cands, eids, flats = [], [], []
        for a in range(S):
            cands.append(v_ref[0, a:a + 1, :] + v2lo)
            eids.append(i_ref[0, a:a + 1, :] * PEER_NKEYS + i2lo)
            flats.append(a * K + sub)
        cands.append(v_ref[0, 0:1, :] + v2hi)
        eids.append(i_ref[0, 0:1, :] * PEER_NKEYS + i2hi)
        flats.append(S + sub)
        cands.append(v_ref[0, S:K, :] + v_ref[1, 0:1, :])
        eids.append(i_ref[0, S:K, :] * PEER_NKEYS + i_ref[1, 0:1, :])
        flats.append((S + sub) * K)
        for i in range(K):
            m = functools.reduce(jnp.maximum, cands)
            m = jnp.max(m, axis=0, keepdims=True)
            pos = functools.reduce(jnp.minimum,
                                   [jnp.where(cd == m, fl, float(K * K)) for cd, fl in zip(cands, flats)])
            pos = jnp.min(pos, axis=0, keepdims=True)
            hits = [fl == pos for fl in flats]
            e = functools.reduce(jnp.add, [jnp.where(ht, ei, 0.0) for ht, ei in zip(hits, eids)])
            etop_ref[i:i + 1, :] = jnp.sum(e, axis=0, keepdims=True)
            top_ref[i:i + 1, :] = m
            cands = [jnp.where(ht, -jnp.inf, cd) for ht, cd in zip(hits, cands)]
        top_s = top_ref[...]
        p = jnp.exp(top_s - top_s[0:1, :])
        g_ref[h * K:(h + 1) * K, :] = p / jnp.sum(p, axis=0, keepdims=True)
        e_ref[h * K:(h + 1) * K, :] = etop_ref[...].astype(jnp.int32)


def _peer_topk(q, subkeys, tb):
    t, d = q.shape
    return pl.pallas_call(
        _peer_topk_kernel,
        out_shape=(jax.ShapeDtypeStruct((PEER_HK, t), jnp.int32),
                   jax.ShapeDtypeStruct((PEER_HK, t), F32)),
        grid=(t // tb,),
        in_specs=[pl.BlockSpec((tb, d), lambda i: (i, 0)),
                  pl.BlockSpec(subkeys.shape, lambda i: (0, 0, 0, 0))],
        out_specs=(pl.BlockSpec((PEER_HK, tb), lambda i: (0, i)),
                   pl.BlockSpec((PEER_HK, tb), lambda i: (0, i))),
        scratch_shapes=[pltpu.VMEM((2, PEER_TOPK, tb), F32),
                        pltpu.VMEM((2, PEER_TOPK, tb), F32),
                        pltpu.VMEM((PEER_TOPK, tb), F32),
                        pltpu.VMEM((PEER_TOPK, tb), F32)],
        compiler_params=_cparams("arbitrary"),
        name="peer_topk",
    )(q, subkeys)


def _gelu(x):
    return 0.5 * x * (1.0 + lax.erf(x * (1.0 / math.sqrt(2.0))))


def _unpack_bf16_pair(w):
    lo = lax.bitcast_convert_type(w << 16, F32)
    hi = lax.bitcast_convert_type(w & jnp.uint32(0xFFFF0000), F32)
    return lo, hi


def _peer_experts_kernel(x_ref, gates_ref, fg_ref, og_ref, ids_hbm, uv_hbm, o_ref,
                         buf0, buf1, buf2, ids_smem, sem, ids_sem):
    i = pl.program_id(0)
    n = pl.num_programs(0)
    sub = PEER_RING_TOKENS
    nsub = x_ref.shape[0] // sub
    nw = x_ref.shape[1] // (2 * LANES)
    rows = sub * PEER_HK
    pitch = PEER_ROW_PITCH
    bufs = (buf0, buf1, buf2)
    nbuf = len(bufs)
    assert nsub == 2 and nbuf == 3, "the ring arithmetic below is written for 2 sub-blocks, 3 buffers"

    def ids_copy(step, slot):
        return pltpu.make_async_copy(ids_hbm.at[step], ids_smem.at[slot], ids_sem.at[slot])

    def issue_token(slot, u, j, dst, dst_sem):
        for k in range(PEER_HK):
            r = j * PEER_HK + k
            e = ids_smem[slot, 0, u * rows + r]
            src = uv_hbm.at[pl.ds(pl.multiple_of(e * (2 * nw), 2 * nw), 2 * nw), :]
            pltpu.make_async_copy(src, dst.at[pl.ds(r * pitch, 2 * nw), :], dst_sem).start(priority=k % 2)

    def wait_all(b):
        window = pl.ds(0, rows * 2 * nw)
        pltpu.make_async_copy(bufs[(b + 1) % nbuf].at[window, :], bufs[b].at[window, :], sem.at[b]).wait()

    def chunk(buf, j, s):
        return buf[pl.ds(j * PEER_HK * pitch + s, PEER_HK, stride=pitch), :]

    @pl.when(i == 0)
    def _():
        ids_copy(0, 0).start()
        ids_copy(jnp.minimum(1, n - 1), 2).start()
        ids_copy(0, 0).wait()
        for u in range(nsub):
            for j in range(sub):
                issue_token(0, u, j, bufs[u], sem.at[u])

    def step(p):
        nxt_slot = (p + 2) % nbuf
        ids_copy(0, nxt_slot).wait()
        ids_copy(jnp.minimum(i + 2, n - 1), (p + 1) % nbuf).start()
        x = x_ref[...]
        hx = _rms(x, fg_ref[...])
        gates = gates_ref[...]
        eye = (lax.broadcasted_iota(jnp.int32, (PEER_HK, PEER_HK), 0)
               == lax.broadcasted_iota(jnp.int32, (PEER_HK, PEER_HK), 1))
        outs = []
        for u in range(nsub):
            b = (p + u) % nbuf
            ahead = (b + 2) % nbuf
            cur = bufs[b]
            wait_all(b)
            for j in range(sub):
                t = u * sub + j
                acc = jnp.zeros((PEER_HK, LANES), F32)
                for s in range(nw):
                    lo, hi = _unpack_bf16_pair(chunk(cur, j, s))
                    acc = acc + lo * hx[t:t + 1, s * LANES:(s + 1) * LANES]
                    acc = acc + hi * hx[t:t + 1, (nw + s) * LANES:(nw + s + 1) * LANES]
                pre = jnp.sum(acc, axis=1, keepdims=True)
                gcol = jnp.sum(jnp.where(eye, gates[t:t + 1, :], 0.0), axis=1, keepdims=True)
                act = _gelu(pre) * gcol
                los, his = [], []
                for s in range(nw):
                    lo, hi = _unpack_bf16_pair(chunk(cur, j, nw + s))
                    los.append(jnp.sum(lo * act, axis=0, keepdims=True))
                    his.append(jnp.sum(hi * act, axis=0, keepdims=True))
                outs.append(jnp.concatenate(los + his, axis=1))
                issue_token(nxt_slot, u, j, bufs[ahead], sem.at[ahead])
        x2 = x + jnp.concatenate(outs, axis=0)
        o_ref[...] = _rms(x2, og_ref[...])

        @pl.when(i == n - 1)
        def _():
            wait_all((p + 2) % nbuf)
            wait_all(p)
            ids_copy(0, (p + 1) % nbuf).wait()

    for p in range(nbuf):
        pl.when(lax.rem(nsub * i, nbuf) == p)(functools.partial(step, p))


def _peer_experts(x1, eidx, gates, ffn_gain, out_gain, uv_tab):
    t, d = x1.shape
    tps = PEER_TOKENS_PER_STEP
    n = t // tps
    rows = PEER_RING_TOKENS * PEER_HK
    buf = pltpu.VMEM((rows * PEER_ROW_PITCH, LANES), jnp.uint32)
    return pl.pallas_call(
        _peer_experts_kernel,
        out_shape=jax.ShapeDtypeStruct((t, d), F32),
        grid=(n,),
        in_specs=[pl.BlockSpec((tps, d), lambda i: (i, 0)),
                  pl.BlockSpec((tps, PEER_HK), lambda i: (i, 0)),
                  pl.BlockSpec((1, d), lambda i: (0, 0)),
                  pl.BlockSpec((1, d), lambda i: (0, 0)),
                  pl.BlockSpec(memory_space=pl.ANY),
                  pl.BlockSpec(memory_space=pl.ANY)],
        out_specs=pl.BlockSpec((tps, d), lambda i: (i, 0)),
        scratch_shapes=[buf, buf, buf,
                        pltpu.SMEM((3, 1, tps * PEER_HK), jnp.int32),
                        pltpu.SemaphoreType.DMA((3,)),
                        pltpu.SemaphoreType.DMA((3,))],
        compiler_params=_cparams("arbitrary"),
        name="peer_experts",
    )(x1, gates, ffn_gain.reshape(1, d), out_gain.reshape(1, d), eidx.reshape(n, 1, tps * PEER_HK), uv_tab)


def _pack_tables_kernel(u_ref, v_ref, o_ref):
    eb, d = u_ref.shape
    nw = d // (2 * LANES)

    def bf16_bits(x):
        return lax.bitcast_convert_type(x.astype(BF16).astype(F32), jnp.uint32)

    for t, ref in enumerate((u_ref, v_ref)):
        for s in range(nw):
            lo = bf16_bits(ref[:, s * LANES:(s + 1) * LANES]) >> 16
            hi = bf16_bits(ref[:, (nw + s) * LANES:(nw + s + 1) * LANES])
            o_ref[pl.ds(t * nw + s, eb, stride=2 * nw), :] = hi | lo


def _merge_expert_tables(u_tab, v_tab):
    ne, d = u_tab.shape
    nw = d // (2 * LANES)
    eb = 256
    return pl.pallas_call(
        _pack_tables_kernel,
        out_shape=jax.ShapeDtypeStruct((ne * 2 * nw, LANES), jnp.uint32),
        grid=(ne // eb,),
        in_specs=[pl.BlockSpec((eb, d), lambda i: (i, 0)),
                  pl.BlockSpec((eb, d), lambda i: (i, 0))],
        out_specs=pl.BlockSpec((eb * 2 * nw, LANES), lambda i: (i, 0)),
        compiler_params=_cparams("arbitrary"),
        name="pack_tables",
    )(u_tab, v_tab)


def _layer(x2d, bsz, seq, attn_norm, w_in, conv_w, conv_b, dt_bias, a_log, d_skip, rel_bias,
           attn_out_norm, ssm_out_norm, w_out, ffn_norm, peer_wq, peer_subkeys, peer_u, peer_v,
           out_gain, buckets, far_bucket):
    n_main = 3 * ATT_WIDTH + 2 * SSM_WIDTH + 2 * SSM_GROUPS * SSM_STATE
    w_main = w_in[:, :n_main].astype(BF16)
    w_dt = jnp.zeros((w_in.shape[0], LANES), F32).at[:, :SSM_HEADS].set(w_in[:, n_main:])
    proj, dt, dtt = _in_proj(x2d, attn_norm, w_main, w_dt, 1024, 512)
    tiles = _bias_tiles(rel_bias, buckets)
    att = _moba(proj, rel_bias, tiles, far_bucket, bsz, seq)
    y = _ssd(proj, dt, dtt, conv_w, conv_b, dt_bias, a_log, d_skip, ssm_out_norm, bsz, seq)
    d_model = x2d.shape[1]
    x1 = _out_proj(att, y, attn_out_norm, x2d, w_out.astype(BF16), 512, d_model)
    pq = _norm_matmul(x1, ffn_norm, peer_wq.astype(BF16), 512, d_model)
    e_t, g_t = _peer_topk(pq, peer_subkeys, LANES)
    uv = _merge_expert_tables(peer_u, peer_v)
    return _peer_experts(x1, e_t.T, g_t.T, ffn_norm, out_gain, uv)


def kernel(x, attn_norm, w_in, conv_w, conv_b, dt_bias, a_log, d_skip, rel_bias, attn_out_norm,
           ssm_out_norm, w_out, ffn_norm, peer_wq, peer_subkeys, peer_u, peer_v, final_norm):
    bsz, seq, d = x.shape
    depth = w_in.shape[0]
    assert depth == 1, "the final rmsnorm is fused into the last layer's expert kernel"
    buckets_np, far_bucket = _bucket_tiles(seq)
    buckets = jnp.asarray(buckets_np)
    out = _layer(x.reshape(bsz * seq, d), bsz, seq, attn_norm[0], w_in[0], conv_w[0], conv_b[0],
                 dt_bias[0], a_log[0], d_skip[0], rel_bias, attn_out_norm[0], ssm_out_norm[0],
                 w_out[0], ffn_norm[0], peer_wq[0], peer_subkeys[0], peer_u[0], peer_v[0],
                 final_norm, buckets, far_bucket)
    return out.reshape(bsz, seq, d)
```

```python
import functools
import math

import numpy as np
import jax
import jax.numpy as jnp
from jax import lax
from jax.experimental import pallas as pl
from jax.experimental.pallas import tpu as pltpu

F32 = jnp.float32
BF16 = jnp.bfloat16
EPS = 1e-6
NEG = -1e30

LANES = 128
SUBLANES = 8
VMEM_LIMIT = 56 * 1024 * 1024

ATT_HEADS = 8
ATT_HEAD_DIM = 128
ATT_WIDTH = ATT_HEADS * ATT_HEAD_DIM
MOBA_BLOCK = 256
MOBA_TOPK = 3
MOBA_HEADS_PER_STEP = 4
REL_BUCKETS = 32
REL_MAX_DIST = 128
SSM_HEADS = 16
SSM_HEAD_DIM = 64
SSM_WIDTH = SSM_HEADS * SSM_HEAD_DIM
SSM_GROUPS = 2
SSM_STATE = 128
SSM_CONV = 4
SSD_CHUNK = 128
PEER_HEADS = 8
PEER_NKEYS = 128
PEER_HALF = 128
PEER_TOPK = 16
PEER_HK = PEER_HEADS * PEER_TOPK
PEER_TOKENS_PER_STEP = 16
PEER_RING_TOKENS = 8
PEER_ROW_PITCH = 20
IN_PROJ_ROWS = 1024
IN_PROJ_COLS = 512
SQUARE_PROJ_ROWS = 512

HIGHEST = lax.Precision.HIGHEST


def _cparams(*sem):
    return pltpu.CompilerParams(dimension_semantics=sem, vmem_limit_bytes=VMEM_LIMIT)


def _rms(x, gain):
    return x * lax.rsqrt(jnp.mean(x * x, axis=-1, keepdims=True) + EPS) * gain


def _dot_nt(a, b, precision=None):
    return lax.dot_general(a, b, (((1,), (1,)), ((), ())), precision=precision,
                           preferred_element_type=F32)


def _in_proj_kernel(x_ref, g_ref, w_ref, wdt_ref, o_ref, dt_ref, dtt_ref, hn_ref):
    @pl.when(pl.program_id(1) == 0)
    def _():
        hn = _rms(x_ref[...], g_ref[...])
        hn_ref[...] = hn.astype(BF16)
        dt = jnp.dot(hn, wdt_ref[...], precision=HIGHEST, preferred_element_type=F32)
        dt_ref[...] = dt
        dtt_ref[...] = dt.T

    o_ref[...] = jnp.dot(hn_ref[...], w_ref[...], preferred_element_type=F32)


def _in_proj(x, gain, w_main, w_dt, tm, tn):
    t, d = x.shape
    n = w_main.shape[1]
    return pl.pallas_call(
        _in_proj_kernel,
        out_shape=(jax.ShapeDtypeStruct((t, n), F32),
                   jax.ShapeDtypeStruct((t, LANES), F32),
                   jax.ShapeDtypeStruct((LANES, t), F32)),
        grid=(t // tm, n // tn),
        in_specs=[pl.BlockSpec((tm, d), lambda i, j: (i, 0)),
                  pl.BlockSpec((1, d), lambda i, j: (0, 0)),
                  pl.BlockSpec((d, tn), lambda i, j: (0, j)),
                  pl.BlockSpec((d, LANES), lambda i, j: (0, 0))],
        out_specs=(pl.BlockSpec((tm, tn), lambda i, j: (i, j)),
                   pl.BlockSpec((tm, LANES), lambda i, j: (i, 0)),
                   pl.BlockSpec((LANES, tm), lambda i, j: (0, i))),
        scratch_shapes=[pltpu.VMEM((tm, d), BF16)],
        compiler_params=_cparams("arbitrary", "arbitrary"),
        name="in_proj",
    )(x, gain.reshape(1, d), w_main, w_dt)


def _t5_bucket_np(rel):
    n = np.maximum(rel, 0)
    max_exact = REL_BUCKETS // 2
    nf = np.maximum(n, 1).astype(np.float64)
    large = max_exact + (np.log(nf / max_exact) / math.log(REL_MAX_DIST / max_exact)
                         * (REL_BUCKETS - max_exact)).astype(np.int32)
    large = np.minimum(large, REL_BUCKETS - 1)
    return np.where(n < max_exact, n, large).astype(np.int32)


def _bucket_tiles(seq):
    ki = np.arange(MOBA_BLOCK)[:, None]
    qi = np.arange(MOBA_BLOCK)[None, :]
    own = _t5_bucket_np(qi - ki)
    prev = _t5_bucket_np(MOBA_BLOCK + qi - ki)
    far = _t5_bucket_np(np.arange(MOBA_BLOCK + 1, max(seq, MOBA_BLOCK + 2)))
    assert (far == far[0]).all(), "bias must be constant beyond the previous block"
    return np.stack([own, prev]), int(far[0])


def _bias_tiles_kernel(rb_ref, bk_ref, o_ref):
    h = pl.program_id(0)
    bk = bk_ref[...]
    acc = jnp.zeros(bk.shape, F32)
    for b in range(REL_BUCKETS):
        acc = jnp.where(bk == b, rb_ref[b, h], acc)
    o_ref[0] = acc


def _bias_tiles(rel_bias, buckets):
    nh = rel_bias.shape[1]
    return pl.pallas_call(
        _bias_tiles_kernel,
        out_shape=jax.ShapeDtypeStruct((nh, 2, MOBA_BLOCK, MOBA_BLOCK), F32),
        grid=(nh,),
        in_specs=[pl.BlockSpec(memory_space=pltpu.SMEM),
                  pl.BlockSpec((2, MOBA_BLOCK, MOBA_BLOCK), lambda h: (0, 0, 0))],
        out_specs=pl.BlockSpec((1, 2, MOBA_BLOCK, MOBA_BLOCK), lambda h: (h, 0, 0, 0)),
        compiler_params=_cparams("arbitrary"),
        name="bias_tiles",
    )(rel_bias, buckets)


def _moba_kernel(far_bucket, rb_ref, q_ref, k_ref, v_ref, bias_ref, o_ref, kmean_ref, kb_ref, vt_ref,
                 sown_ref, s_ref, pen_ref):
    hg = pl.program_id(1)
    c = pl.program_id(2)
    nh = kmean_ref.shape[0]
    nb = kmean_ref.shape[1]
    blk = MOBA_BLOCK
    hd = ATT_HEAD_DIM
    scale = hd ** -0.5

    @pl.when(c == 0)
    def _():
        k = k_ref[...]
        kb_ref[...] = k.astype(BF16)
        vt_ref[...] = jnp.transpose(v_ref[...]).astype(BF16)
        for hh in range(nh):
            kk = k[:, hh * hd:(hh + 1) * hd].reshape(nb, blk, hd)
            kmean_ref[hh] = jnp.sum(kk, axis=1) * (1.0 / blk)

    blk_id = lax.broadcasted_iota(jnp.int32, (nb, blk), 0)
    past = blk_id < c
    key_i = lax.broadcasted_iota(jnp.int32, (blk, blk), 0)
    qry_i = lax.broadcasted_iota(jnp.int32, (blk, blk), 1)
    own_off = pl.multiple_of(c * blk, blk)

    qbs, fars, init = [], [], []
    for hh in range(nh):
        hs = slice(hh * hd, (hh + 1) * hd)
        q = q_ref[:, hs]
        gate = _dot_nt(kmean_ref[hh], q, precision=HIGHEST)
        gm = jnp.where(past, gate, -jnp.inf)
        rank = jnp.zeros(gate.shape, jnp.int32)
        for j in range(nb):
            gj = gm[j:j + 1, :]
            beats = (gj > gm) | ((gj == gm) & (j < blk_id))
            rank = rank + beats.astype(jnp.int32)
        pen_ref[hh] = jnp.where(past & (rank < MOBA_TOPK), 0.0, NEG)
        qb = q.astype(BF16)
        qbs.append(qb)
        fars.append(rb_ref[far_bucket, hg * nh + hh])
        s = _dot_nt(kb_ref[pl.ds(own_off, blk), hs], qb) * scale + bias_ref[hh, 0]
        s = jnp.where(key_i <= qry_i, s, NEG)
        sown_ref[hh] = s
        init.append(jnp.max(s, axis=0, keepdims=True))

    npairs = (c + 1) // 2

    def scores(jp, ms):
        off = pl.multiple_of(jp * (2 * blk), 2 * blk)
        out = []
        for hh in range(nh):
            hs = slice(hh * hd, (hh + 1) * hd)
            sj = _dot_nt(kb_ref[pl.ds(off, 2 * blk), hs], qbs[hh]) * scale
            shift = jnp.concatenate(
                [jnp.where(2 * jp + u == c - 1, bias_ref[hh, 1], fars[hh]) + pen_ref[hh, pl.ds(2 * jp + u, 1), :]
                 for u in range(2)], axis=0)
            sj = sj + shift
            s_ref[hh, pl.ds(off, 2 * blk), :] = sj
            out.append(jnp.maximum(ms[hh], jnp.max(sj, axis=0, keepdims=True)))
        return tuple(out)

    ms = lax.fori_loop(0, npairs, scores, tuple(init))

    start = []
    for hh in range(nh):
        hs = slice(hh * hd, (hh + 1) * hd)
        p = jnp.exp(sown_ref[hh] - ms[hh])
        start.append((jnp.sum(p, axis=0, keepdims=True),
                      jnp.dot(vt_ref[hs, pl.ds(own_off, blk)], p.astype(BF16), preferred_element_type=F32)))

    def values(jp, carry):
        off = pl.multiple_of(jp * (2 * blk), 2 * blk)
        out = []
        for hh in range(nh):
            hs = slice(hh * hd, (hh + 1) * hd)
            l, acc = carry[hh]
            pj = jnp.exp(s_ref[hh, pl.ds(off, 2 * blk), :] - ms[hh])
            pv = jnp.dot(vt_ref[hs, pl.ds(off, 2 * blk)], pj.astype(BF16), preferred_element_type=F32)
            out.append((l + jnp.sum(pj, axis=0, keepdims=True), acc + pv))
        return tuple(out)

    final = lax.fori_loop(0, npairs, values, tuple(start))
    for hh in range(nh):
        l, acc = final[hh]
        o_ref[:, hh * hd:(hh + 1) * hd] = jnp.transpose(acc / l)


def _moba(proj, rel_bias, bias_tiles, far_bucket, bsz, seq):
    nb = seq // MOBA_BLOCK
    t = bsz * seq
    nh = MOBA_HEADS_PER_STEP
    gw = nh * ATT_HEAD_DIM
    ng = ATT_HEADS // nh
    return pl.pallas_call(
        functools.partial(_moba_kernel, far_bucket),
        out_shape=jax.ShapeDtypeStruct((t, ATT_WIDTH), F32),
        grid=(bsz, ng, nb),
        in_specs=[pl.BlockSpec(memory_space=pltpu.SMEM),
                  pl.BlockSpec((MOBA_BLOCK, gw), lambda b, g, c: (b * nb + c, g)),
                  pl.BlockSpec((seq, gw), lambda b, g, c: (b, ng + g)),
                  pl.BlockSpec((seq, gw), lambda b, g, c: (b, 2 * ng + g)),
                  pl.BlockSpec((nh, 2, MOBA_BLOCK, MOBA_BLOCK), lambda b, g, c: (g, 0, 0, 0))],
        out_specs=pl.BlockSpec((MOBA_BLOCK, gw), lambda b, g, c: (b * nb + c, g)),
        scratch_shapes=[pltpu.VMEM((nh, nb, ATT_HEAD_DIM), F32),
                        pltpu.VMEM((seq, gw), BF16),
                        pltpu.VMEM((gw, seq), BF16),
                        pltpu.VMEM((nh, MOBA_BLOCK, MOBA_BLOCK), F32),
                        pltpu.VMEM((nh, seq, MOBA_BLOCK), F32),
                        pltpu.VMEM((nh, nb, MOBA_BLOCK), F32)],
        compiler_params=_cparams("arbitrary", "arbitrary", "arbitrary"),
        name="moba",
    )(rel_bias, proj, proj, proj, bias_tiles)


def _silu(x):
    return x / (1.0 + jnp.exp(-x))


def _softplus(x):
    return jnp.maximum(x, 0.0) + jnp.log(1.0 + jnp.exp(-jnp.abs(x)))


def _ssd_kernel(z_ref, xs_ref, bc_ref, dt_ref, dtt_ref, cwx_ref, cwbc_ref, cbx_ref, cbbc_ref,
                dtb_ref, dtbt_ref, a_ref, at_ref, dskw_ref, gain_ref, exp_ref, o_ref,
                extx_ref, extbc_ref, st_ref, y_ref):
    c = pl.program_id(1)
    L = SSD_CHUNK
    P = SSM_HEAD_DIM
    N = SSM_STATE
    hpg = SSM_HEADS // SSM_GROUPS

    @pl.when(c == 0)
    def _():
        extx_ref[0:SUBLANES, :] = jnp.zeros((SUBLANES, SSM_WIDTH), F32)
        extbc_ref[0:SUBLANES, :] = jnp.zeros((SUBLANES, 2 * SSM_GROUPS * N), F32)
        st_ref[...] = jnp.zeros(st_ref.shape, F32)

    extx_ref[SUBLANES:SUBLANES + L, :] = xs_ref[...]
    extbc_ref[SUBLANES:SUBLANES + L, :] = bc_ref[...]

    def conv(ext_ref, w_ref, b_ref):
        acc = b_ref[...]
        for i in range(SSM_CONV):
            acc = acc + ext_ref[SUBLANES - i:SUBLANES - i + L, :] * w_ref[SSM_CONV - 1 - i:SSM_CONV - i, :]
        return _silu(acc)

    xs = conv(extx_ref, cwx_ref, cbx_ref)
    bcm = conv(extbc_ref, cwbc_ref, cbbc_ref)
    extx_ref[0:SUBLANES, :] = extx_ref[L:L + SUBLANES, :]
    extbc_ref[0:SUBLANES, :] = extbc_ref[L:L + SUBLANES, :]

    dt = _softplus(dt_ref[...] + dtb_ref[...])
    dtt = _softplus(dtt_ref[...] + dtbt_ref[...])
    da = dt * (-jnp.exp(a_ref[...]))
    dat = dtt * (-jnp.exp(at_ref[...]))
    r = lax.broadcasted_iota(jnp.int32, (L, L), 0)
    s_ = lax.broadcasted_iota(jnp.int32, (L, L), 1)
    causal = r >= s_
    tril = jnp.where(causal, 1.0, 0.0)
    triu = jnp.where(r <= s_, 1.0, 0.0)
    cs = jnp.dot(tril, da, precision=HIGHEST, preferred_element_type=F32)
    cst = jnp.dot(dat, triu, precision=HIGHEST, preferred_element_type=F32)
    ecs = jnp.exp(cs)
    cs_last = cs[L - 1:L, :]
    dst = jnp.exp(cs_last - cs)

    cols = jnp.concatenate([dt, ecs, dst], axis=0)
    expand = exp_ref[...]
    wide = jnp.zeros((3 * L, SSM_WIDTH), F32)
    rest = cols
    for _ in range(3):
        piece = rest.astype(BF16)
        wide = wide + jnp.dot(piece, expand, preferred_element_type=F32)
        rest = rest - piece.astype(F32)
    dt_w, ecs_w, dst_w = wide[0:L], wide[L:2 * L], wide[2 * L:3 * L]
    xdt_w = xs * dt_w
    xdt_b = xdt_w.astype(BF16)
    xend_b = (xdt_w * dst_w).astype(BF16)
    skip_w = xs * dskw_ref[...]

    for g in range(SSM_GROUPS):
        bg = bcm[:, g * N:(g + 1) * N]
        cg = bcm[:, (SSM_GROUPS + g) * N:(SSM_GROUPS + g + 1) * N]
        bgb = bg.astype(BF16)
        cgb = cg.astype(BF16)
        cb = _dot_nt(cgb, bgb)
        bgt = jnp.transpose(bg).astype(BF16)
        for hh in range(hpg):
            h = g * hpg + hh
            hp = slice(h * P, (h + 1) * P)
            seg = cs[:, h:h + 1] - cst[h:h + 1, :]
            decay = jnp.exp(jnp.where(causal, seg, NEG))
            y = jnp.dot((cb * decay).astype(BF16), xdt_b[:, hp], preferred_element_type=F32)
            st = st_ref[h]
            y = y + jnp.dot(cgb, st.astype(BF16), preferred_element_type=F32) * ecs_w[:, hp]
            y_ref[:, hp] = y + skip_w[:, hp]
            new = jnp.dot(bgt, xend_b[:, hp], preferred_element_type=F32)
            st_ref[h] = st * ecs_w[L - 1:L, hp] + new

    y = y_ref[...] * _silu(z_ref[...])
    gw = SSM_WIDTH // SSM_GROUPS
    for g in range(SSM_GROUPS):
        yg = y[:, g * gw:(g + 1) * gw]
        o_ref[:, g * gw:(g + 1) * gw] = _rms(yg, gain_ref[:, g * gw:(g + 1) * gw]).astype(o_ref.dtype)


def _pad_lanes(v):
    return jnp.zeros((1, LANES), F32).at[0, :v.shape[0]].set(v.astype(F32))


def _ssd(proj, dt, dtt, conv_w, conv_b, dt_bias, a_log, d_skip, gain, bsz, seq):
    t = bsz * seq
    nc = seq // SSD_CHUNK
    L = SSD_CHUNK
    nbc = 2 * SSM_GROUPS * SSM_STATE
    zcol = 3 * ATT_WIDTH // SSM_WIDTH
    xcol = zcol + 1
    bccol = (3 * ATT_WIDTH + 2 * SSM_WIDTH) // nbc
    dtb = _pad_lanes(dt_bias)
    alog = jnp.full((1, LANES), -jnp.inf, F32).at[0, :SSM_HEADS].set(a_log.astype(F32))
    dsk_wide = jnp.repeat(d_skip.astype(F32), SSM_HEAD_DIM).reshape(1, SSM_WIDTH)
    expand = jnp.asarray(np.arange(LANES)[:, None] == np.arange(SSM_WIDTH)[None, :] // SSM_HEAD_DIM, BF16)
    full = lambda shape: pl.BlockSpec(shape, lambda b, c: tuple(0 for _ in shape))
    return pl.pallas_call(
        _ssd_kernel,
        out_shape=jax.ShapeDtypeStruct((t, SSM_WIDTH), BF16),
        grid=(bsz, nc),
        in_specs=[pl.BlockSpec((L, SSM_WIDTH), lambda b, c: (b * nc + c, zcol)),
                  pl.BlockSpec((L, SSM_WIDTH), lambda b, c: (b * nc + c, xcol)),
                  pl.BlockSpec((L, nbc), lambda b, c: (b * nc + c, bccol)),
                  pl.BlockSpec((L, LANES), lambda b, c: (b * nc + c, 0)),
                  pl.BlockSpec((LANES, L), lambda b, c: (0, b * nc + c)),
                  full((SSM_CONV, SSM_WIDTH)), full((SSM_CONV, nbc)),
                  full((1, SSM_WIDTH)), full((1, nbc)),
                  full((1, LANES)), full((LANES, 1)), full((1, LANES)), full((LANES, 1)),
                  full((1, SSM_WIDTH)), full((1, SSM_WIDTH)), full((LANES, SSM_WIDTH))],
        out_specs=pl.BlockSpec((L, SSM_WIDTH), lambda b, c: (b * nc + c, 0)),
        scratch_shapes=[pltpu.VMEM((L + 2 * SUBLANES, SSM_WIDTH), F32),
                        pltpu.VMEM((L + 2 * SUBLANES, nbc), F32),
                        pltpu.VMEM((SSM_HEADS, SSM_STATE, SSM_HEAD_DIM), F32),
                        pltpu.VMEM((L, SSM_WIDTH), F32)],
        compiler_params=_cparams("arbitrary", "arbitrary"),
        name="ssd",
    )(proj, proj, proj, dt, dtt,
      conv_w[:, :SSM_WIDTH], conv_w[:, SSM_WIDTH:],
      conv_b[:SSM_WIDTH].reshape(1, -1), conv_b[SSM_WIDTH:].reshape(1, -1),
      dtb, dtb.reshape(LANES, 1), alog, alog.reshape(LANES, 1), dsk_wide, gain.reshape(1, -1), expand)


def _out_q_kernel(att_ref, y_ref, g_ref, x_ref, w_ref, fg_ref, wq_ref, x1_ref, pq_ref, mix_ref):
    mix_ref[:, :ATT_WIDTH] = _rms(att_ref[...], g_ref[...]).astype(BF16)
    mix_ref[:, ATT_WIDTH:] = y_ref[...]
    x1 = x_ref[...] + jnp.dot(mix_ref[...], w_ref[...], preferred_element_type=F32)
    x1_ref[...] = x1
    pq_ref[...] = jnp.dot(_rms(x1, fg_ref[...]).astype(BF16), wq_ref[...], preferred_element_type=F32)


def _out_q(att, y, gain, x, w_out, ffn_gain, wq, tm):
    t, d = x.shape
    k = w_out.shape[0]
    nq = wq.shape[1]
    row = lambda width: pl.BlockSpec((tm, width), lambda i: (i, 0))
    const = lambda shape: pl.BlockSpec(shape, lambda i: (0, 0))
    weight = lambda shape: pl.BlockSpec(shape, lambda i: (0, 0), pipeline_mode=pl.Buffered(1))
    return pl.pallas_call(
        _out_q_kernel,
        out_shape=(jax.ShapeDtypeStruct((t, d), F32), jax.ShapeDtypeStruct((t, nq), F32)),
        grid=(t // tm,),
        in_specs=[row(ATT_WIDTH), row(SSM_WIDTH), const((1, ATT_WIDTH)), row(d), weight((k, d)),
                  const((1, d)), weight((d, nq))],
        out_specs=(row(d), row(nq)),
        scratch_shapes=[pltpu.VMEM((tm, k), BF16)],
        compiler_params=_cparams("arbitrary"),
        name="out_q",
    )(att, y, gain.reshape(1, -1), x, w_out, ffn_gain.reshape(1, -1), wq)


def _peer_topk_kernel(q_ref, sk_ref, e_ref, g_ref, v_ref, i_ref, top_ref, etop_ref):
    tb = q_ref.shape[0]
    K = PEER_TOPK
    S = SUBLANES
    key_iota = lax.broadcasted_iota(jnp.int32, (PEER_NKEYS, tb), 0).astype(F32)
    sub = lax.broadcasted_iota(jnp.int32, (S, tb), 0).astype(F32)
    for h in range(PEER_HEADS):
        for c in range(2):
            col = (2 * h + c) * PEER_HALF
            sc = _dot_nt(sk_ref[h, c], q_ref[:, col:col + PEER_HALF], precision=HIGHEST)
            for i in range(K):
                m = jnp.max(sc, axis=0, keepdims=True)
                idx = jnp.min(jnp.where(sc == m, key_iota, float(PEER_NKEYS)), axis=0, keepdims=True)
                v_ref[c, i:i + 1, :] = m
                i_ref[c, i:i + 1, :] = idx
                sc = jnp.where(key_iota == idx, -jnp.inf, sc)
        v2lo, v2hi = v_ref[1, 0:S, :], v_ref[1, S:K, :]
        i2lo, i2hi = i_ref[1, 0:S, :], i_ref[1, S:K, :]
        cands, eids, flats = [], [], []
        for a in range(S):
            cands.append(v_ref[0, a:a + 1, :] + v2lo)
            eids.append(i_ref[0, a:a + 1, :] * PEER_NKEYS + i2lo)
            flats.append(a * K + sub)
        cands.append(v_ref[0, 0:1, :] + v2hi)
        eids.append(i_ref[0, 0:1, :] * PEER_NKEYS + i2hi)
        flats.append(S + sub)
        cands.append(v_ref[0, S:K, :] + v_ref[1, 0:1, :])
        eids.append(i_ref[0, S:K, :] * PEER_NKEYS + i_ref[1, 0:1, :])
        flats.append((S + sub) * K)
        for i in range(K):
            m = functools.reduce(jnp.maximum, cands)
            m = jnp.max(m, axis=0, keepdims=True)
            pos = functools.reduce(jnp.minimum,
                                   [jnp.where(cd == m, fl, float(K * K)) for cd, fl in zip(cands, flats)])
            pos = jnp.min(pos, axis=0, keepdims=True)
            hits = [fl == pos for fl in flats]
            e = functools.reduce(jnp.add, [jnp.where(ht, ei, 0.0) for ht, ei in zip(hits, eids)])
            etop_ref[i:i + 1, :] = jnp.sum(e, axis=0, keepdims=True)
            top_ref[i:i + 1, :] = m
            cands = [jnp.where(ht, -jnp.inf, cd) for ht, cd in zip(hits, cands)]
        top_s = top_ref[...]
        p = jnp.exp(top_s - top_s[0:1, :])
        g_ref[h * K:(h + 1) * K, :] = p / jnp.sum(p, axis=0, keepdims=True)
        e_ref[h * K:(h + 1) * K, :] = etop_ref[...].astype(jnp.int32)


def _peer_topk(q, subkeys, tb):
    t, d = q.shape
    return pl.pallas_call(
        _peer_topk_kernel,
        out_shape=(jax.ShapeDtypeStruct((PEER_HK, t), jnp.int32),
                   jax.ShapeDtypeStruct((PEER_HK, t), F32)),
        grid=(t // tb,),
        in_specs=[pl.BlockSpec((tb, d), lambda i: (i, 0)),
                  pl.BlockSpec(subkeys.shape, lambda i: (0, 0, 0, 0))],
        out_specs=(pl.BlockSpec((PEER_HK, tb), lambda i: (0, i)),
                   pl.BlockSpec((PEER_HK, tb), lambda i: (0, i))),
        scratch_shapes=[pltpu.VMEM((2, PEER_TOPK, tb), F32),
                        pltpu.VMEM((2, PEER_TOPK, tb), F32),
                        pltpu.VMEM((PEER_TOPK, tb), F32),
                        pltpu.VMEM((PEER_TOPK, tb), F32)],
        compiler_params=_cparams("arbitrary"),
        name="peer_topk",
    )(q, subkeys)


def _gelu(x):
    return 0.5 * x * (1.0 + lax.erf(x * (1.0 / math.sqrt(2.0))))


def _unpack_bf16_pair(w):
    lo = lax.bitcast_convert_type(w << 16, F32)
    hi = lax.bitcast_convert_type(w & jnp.uint32(0xFFFF0000), F32)
    return lo, hi


def _peer_experts_kernel(x_ref, gates_ref, fg_ref, og_ref, ids_hbm, uv_hbm, o_ref,
                         buf0, buf1, buf2, ids_smem, sem, ids_sem):
    i = pl.program_id(0)
    n = pl.num_programs(0)
    sub = PEER_RING_TOKENS
    nsub = x_ref.shape[0] // sub
    nw = x_ref.shape[1] // (2 * LANES)
    rows = sub * PEER_HK
    pitch = PEER_ROW_PITCH
    bufs = (buf0, buf1, buf2)
    nbuf = len(bufs)
    assert nsub == 2 and nbuf == 3, "the ring arithmetic below is written for 2 sub-blocks, 3 buffers"

    def ids_copy(step, slot):
        return pltpu.make_async_copy(ids_hbm.at[step], ids_smem.at[slot], ids_sem.at[slot])

    def issue_token(slot, u, j, dst, dst_sem):
        for k in range(PEER_HK):
            r = j * PEER_HK + k
            e = ids_smem[slot, 0, u * rows + r]
            src = uv_hbm.at[pl.ds(pl.multiple_of(e * (2 * nw), 2 * nw), 2 * nw), :]
            pltpu.make_async_copy(src, dst.at[pl.ds(r * pitch, 2 * nw), :], dst_sem).start(priority=k % 2)

    def wait_all(b):
        window = pl.ds(0, rows * 2 * nw)
        pltpu.make_async_copy(bufs[(b + 1) % nbuf].at[window, :], bufs[b].at[window, :], sem.at[b]).wait()

    def chunk(buf, j, s):
        return buf[pl.ds(j * PEER_HK * pitch + s, PEER_HK, stride=pitch), :]

    @pl.when(i == 0)
    def _():
        ids_copy(0, 0).start()
        ids_copy(jnp.minimum(1, n - 1), 2).start()
        ids_copy(0, 0).wait()
        for u in range(nsub):
            for j in range(sub):
                issue_token(0, u, j, bufs[u], sem.at[u])

    def step(p):
        nxt_slot = (p + 2) % nbuf
        ids_copy(0, nxt_slot).wait()
        ids_copy(jnp.minimum(i + 2, n - 1), (p + 1) % nbuf).start()
        x = x_ref[...]
        hx = _rms(x, fg_ref[...])
        gates = gates_ref[...]
        eye = (lax.broadcasted_iota(jnp.int32, (PEER_HK, PEER_HK), 0)
               == lax.broadcasted_iota(jnp.int32, (PEER_HK, PEER_HK), 1))
        outs = []
        for u in range(nsub):
            b = (p + u) % nbuf
            ahead = (b + 2) % nbuf
            cur = bufs[b]
            wait_all(b)
            for j in range(sub):
                t = u * sub + j
                acc = jnp.zeros((PEER_HK, LANES), F32)
                for s in range(nw):
                    lo, hi = _unpack_bf16_pair(chunk(cur, j, s))
                    acc = acc + lo * hx[t:t + 1, s * LANES:(s + 1) * LANES]
                    acc = acc + hi * hx[t:t + 1, (nw + s) * LANES:(nw + s + 1) * LANES]
                pre = jnp.sum(acc, axis=1, keepdims=True)
                gcol = jnp.sum(jnp.where(eye, gates[t:t + 1, :], 0.0), axis=1, keepdims=True)
                act = _gelu(pre) * gcol
                los, his = [], []
                for s in range(nw):
                    lo, hi = _unpack_bf16_pair(chunk(cur, j, nw + s))
                    los.append(jnp.sum(lo * act, axis=0, keepdims=True))
                    his.append(jnp.sum(hi * act, axis=0, keepdims=True))
                outs.append(jnp.concatenate(los + his, axis=1))
                issue_token(nxt_slot, u, j, bufs[ahead], sem.at[ahead])
        x2 = x + jnp.concatenate(outs, axis=0)
        o_ref[...] = _rms(x2, og_ref[...])

        @pl.when(i == n - 1)
        def _():
            wait_all((p + 2) % nbuf)
            wait_all(p)
            ids_copy(0, (p + 1) % nbuf).wait()

    for p in range(nbuf):
        pl.when(lax.rem(nsub * i, nbuf) == p)(functools.partial(step, p))


def _peer_experts(x1, eidx, gates, ffn_gain, out_gain, uv_tab):
    t, d = x1.shape
    tps = PEER_TOKENS_PER_STEP
    n = t // tps
    rows = PEER_RING_TOKENS * PEER_HK
    buf = pltpu.VMEM((rows * PEER_ROW_PITCH, LANES), jnp.uint32)
    return pl.pallas_call(
        _peer_experts_kernel,
        out_shape=jax.ShapeDtypeStruct((t, d), F32),
        grid=(n,),
        in_specs=[pl.BlockSpec((tps, d), lambda i: (i, 0)),
                  pl.BlockSpec((tps, PEER_HK), lambda i: (i, 0)),
                  pl.BlockSpec((1, d), lambda i: (0, 0)),
                  pl.BlockSpec((1, d), lambda i: (0, 0)),
                  pl.BlockSpec(memory_space=pl.ANY),
                  pl.BlockSpec(memory_space=pl.ANY)],
        out_specs=pl.BlockSpec((tps, d), lambda i: (i, 0)),
        scratch_shapes=[buf, buf, buf,
                        pltpu.SMEM((3, 1, tps * PEER_HK), jnp.int32),
                        pltpu.SemaphoreType.DMA((3,)),
                        pltpu.SemaphoreType.DMA((3,))],
        compiler_params=_cparams("arbitrary"),
        name="peer_experts",
    )(x1, gates, ffn_gain.reshape(1, d), out_gain.reshape(1, d), eidx.reshape(n, 1, tps * PEER_HK), uv_tab)


def _pack_tables_kernel(u_ref, v_ref, o_ref):
    eb, d = u_ref.shape
    nw = d // (2 * LANES)

    def bf16_bits(x):
        return lax.bitcast_convert_type(x.astype(BF16).astype(F32), jnp.uint32)

    for t, ref in enumerate((u_ref, v_ref)):
        for s in range(nw):
            lo = bf16_bits(ref[:, s * LANES:(s + 1) * LANES]) >> 16
            hi = bf16_bits(ref[:, (nw + s) * LANES:(nw + s + 1) * LANES])
            o_ref[pl.ds(t * nw + s, eb, stride=2 * nw), :] = hi | lo


def _merge_expert_tables(u_tab, v_tab):
    ne, d = u_tab.shape
    nw = d // (2 * LANES)
    eb = 256
    return pl.pallas_call(
        _pack_tables_kernel,
        out_shape=jax.ShapeDtypeStruct((ne * 2 * nw, LANES), jnp.uint32),
        grid=(ne // eb,),
        in_specs=[pl.BlockSpec((eb, d), lambda i: (i, 0)),
                  pl.BlockSpec((eb, d), lambda i: (i, 0))],
        out_specs=pl.BlockSpec((eb * 2 * nw, LANES), lambda i: (i, 0)),
        compiler_params=_cparams("arbitrary"),
        name="pack_tables",
    )(u_tab, v_tab)


def _layer(x2d, bsz, seq, attn_norm, w_in, conv_w, conv_b, dt_bias, a_log, d_skip, rel_bias,
           attn_out_norm, ssm_out_norm, w_out, ffn_norm, peer_wq, peer_subkeys, peer_u, peer_v,
           out_gain, buckets, far_bucket):
    n_main = 3 * ATT_WIDTH + 2 * SSM_WIDTH + 2 * SSM_GROUPS * SSM_STATE
    w_main = w_in[:, :n_main].astype(BF16)
    w_dt = jnp.zeros((w_in.shape[0], LANES), F32).at[:, :SSM_HEADS].set(w_in[:, n_main:])
    proj, dt, dtt = _in_proj(x2d, attn_norm, w_main, w_dt, IN_PROJ_ROWS, IN_PROJ_COLS)
    tiles = _bias_tiles(rel_bias, buckets)
    att = _moba(proj, rel_bias, tiles, far_bucket, bsz, seq)
    y = _ssd(proj, dt, dtt, conv_w, conv_b, dt_bias, a_log, d_skip, ssm_out_norm, bsz, seq)
    x1, pq = _out_q(att, y, attn_out_norm, x2d, w_out.astype(BF16), ffn_norm, peer_wq.astype(BF16),
                    SQUARE_PROJ_ROWS)
    e_t, g_t = _peer_topk(pq, peer_subkeys, LANES)
    uv = _merge_expert_tables(peer_u, peer_v)
    return _peer_experts(x1, e_t.T, g_t.T, ffn_norm, out_gain, uv)


def kernel(x, attn_norm, w_in, conv_w, conv_b, dt_bias, a_log, d_skip, rel_bias, attn_out_norm,
           ssm_out_norm, w_out, ffn_norm, peer_wq, peer_subkeys, peer_u, peer_v, final_norm):
    bsz, seq, d = x.shape
    depth = w_in.shape[0]
    assert depth == 1, "the final rmsnorm is fused into the last layer's expert kernel"
    buckets_np, far_bucket = _bucket_tiles(seq)
    buckets = jnp.asarray(buckets_np)
    out = _layer(x.reshape(bsz * seq, d), bsz, seq, attn_norm[0], w_in[0], conv_w[0], conv_b[0],
                 dt_bias[0], a_log[0], d_skip[0], rel_bias, attn_out_norm[0], ssm_out_norm[0],
                 w_out[0], ffn_norm[0], peer_wq[0], peer_subkeys[0], peer_u[0], peer_v[0],
                 final_norm, buckets, far_bucket)
    return out.reshape(bsz, seq, d)
```

```python
import functools
import math

import numpy as np
import jax
import jax.numpy as jnp
from jax import lax
from jax.experimental import pallas as pl
from jax.experimental.pallas import tpu as pltpu

F32 = jnp.float32
BF16 = jnp.bfloat16
EPS = 1e-6
NEG = -1e30

LANES = 128
SUBLANES = 8
VMEM_LIMIT = 56 * 1024 * 1024

ATT_HEADS = 8
ATT_HEAD_DIM = 128
ATT_WIDTH = ATT_HEADS * ATT_HEAD_DIM
MOBA_BLOCK = 256
MOBA_TOPK = 3
MOBA_HEADS_PER_STEP = 4
REL_BUCKETS = 32
REL_MAX_DIST = 128
SSM_HEADS = 16
SSM_HEAD_DIM = 64
SSM_WIDTH = SSM_HEADS * SSM_HEAD_DIM
SSM_GROUPS = 2
SSM_STATE = 128
SSM_CONV = 4
SSD_CHUNK = 128
PEER_HEADS = 8
PEER_NKEYS = 128
PEER_HALF = 128
PEER_TOPK = 16
PEER_HK = PEER_HEADS * PEER_TOPK
PEER_TOKENS_PER_STEP = 16
PEER_RING_TOKENS = 8
PEER_ROW_PITCH = 20
IN_PROJ_ROWS = 1024
IN_PROJ_COLS = 512
SQUARE_PROJ_ROWS = 512

HIGHEST = lax.Precision.HIGHEST


def _cparams(*sem):
    return pltpu.CompilerParams(dimension_semantics=sem, vmem_limit_bytes=VMEM_LIMIT)


def _rms(x, gain):
    return x * lax.rsqrt(jnp.mean(x * x, axis=-1, keepdims=True) + EPS) * gain


def _dot_nt(a, b, precision=None):
    return lax.dot_general(a, b, (((1,), (1,)), ((), ())), precision=precision,
                           preferred_element_type=F32)


def _in_proj_kernel(x_ref, g_ref, w_ref, wdt_ref, o_ref, dt_ref, dtt_ref, hn_ref):
    @pl.when(pl.program_id(1) == 0)
    def _():
        hn = _rms(x_ref[...], g_ref[...])
        hn_ref[...] = hn.astype(BF16)
        dt = jnp.dot(hn, wdt_ref[...], precision=HIGHEST, preferred_element_type=F32)
        dt_ref[...] = dt
        dtt_ref[...] = dt.T

    o_ref[...] = jnp.dot(hn_ref[...], w_ref[...], preferred_element_type=F32)


def _in_proj(x, gain, w_main, w_dt, tm, tn):
    t, d = x.shape
    n = w_main.shape[1]
    return pl.pallas_call(
        _in_proj_kernel,
        out_shape=(jax.ShapeDtypeStruct((t, n), F32),
                   jax.ShapeDtypeStruct((t, LANES), F32),
                   jax.ShapeDtypeStruct((LANES, t), F32)),
        grid=(t // tm, n // tn),
        in_specs=[pl.BlockSpec((tm, d), lambda i, j: (i, 0)),
                  pl.BlockSpec((1, d), lambda i, j: (0, 0)),
                  pl.BlockSpec((d, tn), lambda i, j: (0, j)),
                  pl.BlockSpec((d, LANES), lambda i, j: (0, 0))],
        out_specs=(pl.BlockSpec((tm, tn), lambda i, j: (i, j)),
                   pl.BlockSpec((tm, LANES), lambda i, j: (i, 0)),
                   pl.BlockSpec((LANES, tm), lambda i, j: (0, i))),
        scratch_shapes=[pltpu.VMEM((tm, d), BF16)],
        compiler_params=_cparams("arbitrary", "arbitrary"),
        name="in_proj",
    )(x, gain.reshape(1, d), w_main, w_dt)


def _t5_bucket_np(rel):
    n = np.maximum(rel, 0)
    max_exact = REL_BUCKETS // 2
    nf = np.maximum(n, 1).astype(np.float64)
    large = max_exact + (np.log(nf / max_exact) / math.log(REL_MAX_DIST / max_exact)
                         * (REL_BUCKETS - max_exact)).astype(np.int32)
    large = np.minimum(large, REL_BUCKETS - 1)
    return np.where(n < max_exact, n, large).astype(np.int32)


def _bucket_tiles(seq):
    ki = np.arange(MOBA_BLOCK)[:, None]
    qi = np.arange(MOBA_BLOCK)[None, :]
    own = _t5_bucket_np(qi - ki)
    prev = _t5_bucket_np(MOBA_BLOCK + qi - ki)
    far = _t5_bucket_np(np.arange(MOBA_BLOCK + 1, max(seq, MOBA_BLOCK + 2)))
    assert (far == far[0]).all(), "bias must be constant beyond the previous block"
    return np.stack([own, prev]), int(far[0])


def _bias_tiles_kernel(rb_ref, bk_ref, o_ref):
    h = pl.program_id(0)
    bk = bk_ref[...]
    acc = jnp.zeros(bk.shape, F32)
    for b in range(REL_BUCKETS):
        acc = jnp.where(bk == b, rb_ref[b, h], acc)
    o_ref[0] = acc


def _bias_tiles(rel_bias, buckets):
    nh = rel_bias.shape[1]
    return pl.pallas_call(
        _bias_tiles_kernel,
        out_shape=jax.ShapeDtypeStruct((nh, 2, MOBA_BLOCK, MOBA_BLOCK), F32),
        grid=(nh,),
        in_specs=[pl.BlockSpec(memory_space=pltpu.SMEM),
                  pl.BlockSpec((2, MOBA_BLOCK, MOBA_BLOCK), lambda h: (0, 0, 0))],
        out_specs=pl.BlockSpec((1, 2, MOBA_BLOCK, MOBA_BLOCK), lambda h: (h, 0, 0, 0)),
        compiler_params=_cparams("arbitrary"),
        name="bias_tiles",
    )(rel_bias, buckets)


def _moba_kernel(far_bucket, rb_ref, q_ref, k_ref, v_ref, bias_ref, o_ref, kmean_ref, kb_ref, vt_ref,
                 sown_ref, s_ref, pen_ref):
    hg = pl.program_id(1)
    c = pl.program_id(2)
    nh = kmean_ref.shape[0]
    nb = kmean_ref.shape[1]
    blk = MOBA_BLOCK
    hd = ATT_HEAD_DIM
    scale = hd ** -0.5

    @pl.when(c == 0)
    def _():
        k = k_ref[...]
        kb_ref[...] = k.astype(BF16)
        vt_ref[...] = jnp.transpose(v_ref[...]).astype(BF16)
        for hh in range(nh):
            kk = k[:, hh * hd:(hh + 1) * hd].reshape(nb, blk, hd)
            kmean_ref[hh] = jnp.sum(kk, axis=1) * (1.0 / blk)

    blk_id = lax.broadcasted_iota(jnp.int32, (nb, blk), 0)
    past = blk_id < c
    key_i = lax.broadcasted_iota(jnp.int32, (blk, blk), 0)
    qry_i = lax.broadcasted_iota(jnp.int32, (blk, blk), 1)
    own_off = pl.multiple_of(c * blk, blk)

    qbs, fars, init = [], [], []
    for hh in range(nh):
        hs = slice(hh * hd, (hh + 1) * hd)
        q = q_ref[:, hs]
        gate = _dot_nt(kmean_ref[hh], q, precision=HIGHEST)
        gm = jnp.where(past, gate, -jnp.inf)
        rank = jnp.zeros(gate.shape, jnp.int32)
        for j in range(nb):
            gj = gm[j:j + 1, :]
            beats = (gj > gm) | ((gj == gm) & (j < blk_id))
            rank = rank + beats.astype(jnp.int32)
        pen_ref[hh] = jnp.where(past & (rank < MOBA_TOPK), 0.0, NEG)
        qb = q.astype(BF16)
        qbs.append(qb)
        fars.append(rb_ref[far_bucket, hg * nh + hh])
        s = _dot_nt(kb_ref[pl.ds(own_off, blk), hs], qb) * scale + bias_ref[hh, 0]
        s = jnp.where(key_i <= qry_i, s, NEG)
        sown_ref[hh] = s
        init.append(jnp.max(s, axis=0, keepdims=True))

    npairs = (c + 1) // 2

    def scores(jp, ms):
        off = pl.multiple_of(jp * (2 * blk), 2 * blk)
        out = []
        for hh in range(nh):
            hs = slice(hh * hd, (hh + 1) * hd)
            sj = _dot_nt(kb_ref[pl.ds(off, 2 * blk), hs], qbs[hh]) * scale
            shift = jnp.concatenate(
                [jnp.where(2 * jp + u == c - 1, bias_ref[hh, 1], fars[hh]) + pen_ref[hh, pl.ds(2 * jp + u, 1), :]
                 for u in range(2)], axis=0)
            sj = sj + shift
            s_ref[hh, pl.ds(off, 2 * blk), :] = sj
            out.append(jnp.maximum(ms[hh], jnp.max(sj, axis=0, keepdims=True)))
        return tuple(out)

    ms = lax.fori_loop(0, npairs, scores, tuple(init))

    start = []
    for hh in range(nh):
        hs = slice(hh * hd, (hh + 1) * hd)
        p = jnp.exp(sown_ref[hh] - ms[hh])
        start.append((jnp.sum(p, axis=0, keepdims=True),
                      jnp.dot(vt_ref[hs, pl.ds(own_off, blk)], p.astype(BF16), preferred_element_type=F32)))

    def values(jp, carry):
        off = pl.multiple_of(jp * (2 * blk), 2 * blk)
        out = []
        for hh in range(nh):
            hs = slice(hh * hd, (hh + 1) * hd)
            l, acc = carry[hh]
            pj = jnp.exp(s_ref[hh, pl.ds(off, 2 * blk), :] - ms[hh])
            pv = jnp.dot(vt_ref[hs, pl.ds(off, 2 * blk)], pj.astype(BF16), preferred_element_type=F32)
            out.append((l + jnp.sum(pj, axis=0, keepdims=True), acc + pv))
        return tuple(out)

    final = lax.fori_loop(0, npairs, values, tuple(start))
    for hh in range(nh):
        l, acc = final[hh]
        o_ref[:, hh * hd:(hh + 1) * hd] = jnp.transpose(acc / l)


def _moba(proj, rel_bias, bias_tiles, far_bucket, bsz, seq):
    nb = seq // MOBA_BLOCK
    t = bsz * seq
    nh = MOBA_HEADS_PER_STEP
    gw = nh * ATT_HEAD_DIM
    ng = ATT_HEADS // nh
    return pl.pallas_call(
        functools.partial(_moba_kernel, far_bucket),
        out_shape=jax.ShapeDtypeStruct((t, ATT_WIDTH), F32),
        grid=(bsz, ng, nb),
        in_specs=[pl.BlockSpec(memory_space=pltpu.SMEM),
                  pl.BlockSpec((MOBA_BLOCK, gw), lambda b, g, c: (b * nb + c, g)),
                  pl.BlockSpec((seq, gw), lambda b, g, c: (b, ng + g)),
                  pl.BlockSpec((seq, gw), lambda b, g, c: (b, 2 * ng + g)),
                  pl.BlockSpec((nh, 2, MOBA_BLOCK, MOBA_BLOCK), lambda b, g, c: (g, 0, 0, 0))],
        out_specs=pl.BlockSpec((MOBA_BLOCK, gw), lambda b, g, c: (b * nb + c, g)),
        scratch_shapes=[pltpu.VMEM((nh, nb, ATT_HEAD_DIM), F32),
                        pltpu.VMEM((seq, gw), BF16),
                        pltpu.VMEM((gw, seq), BF16),
                        pltpu.VMEM((nh, MOBA_BLOCK, MOBA_BLOCK), F32),
                        pltpu.VMEM((nh, seq, MOBA_BLOCK), F32),
                        pltpu.VMEM((nh, nb, MOBA_BLOCK), F32)],
        compiler_params=_cparams("arbitrary", "arbitrary", "arbitrary"),
        name="moba",
    )(rel_bias, proj, proj, proj, bias_tiles)


def _silu(x):
    return x / (1.0 + jnp.exp(-x))


def _softplus(x):
    return jnp.maximum(x, 0.0) + jnp.log(1.0 + jnp.exp(-jnp.abs(x)))


def _ssd_kernel(z_ref, xs_ref, bc_ref, dt_ref, dtt_ref, cwx_ref, cwbc_ref, cbx_ref, cbbc_ref,
                dtb_ref, dtbt_ref, a_ref, at_ref, dskw_ref, gain_ref, exp_ref, o_ref,
                extx_ref, extbc_ref, st_ref, y_ref):
    c = pl.program_id(1)
    L = SSD_CHUNK
    P = SSM_HEAD_DIM
    N = SSM_STATE
    hpg = SSM_HEADS // SSM_GROUPS

    @pl.when(c == 0)
    def _():
        extx_ref[0:SUBLANES, :] = jnp.zeros((SUBLANES, SSM_WIDTH), F32)
        extbc_ref[0:SUBLANES, :] = jnp.zeros((SUBLANES, 2 * SSM_GROUPS * N), F32)
        st_ref[...] = jnp.zeros(st_ref.shape, F32)

    extx_ref[SUBLANES:SUBLANES + L, :] = xs_ref[...]
    extbc_ref[SUBLANES:SUBLANES + L, :] = bc_ref[...]

    def conv(ext_ref, w_ref, b_ref):
        acc = b_ref[...]
        for i in range(SSM_CONV):
            acc = acc + ext_ref[SUBLANES - i:SUBLANES - i + L, :] * w_ref[SSM_CONV - 1 - i:SSM_CONV - i, :]
        return _silu(acc)

    xs = conv(extx_ref, cwx_ref, cbx_ref)
    bcm = conv(extbc_ref, cwbc_ref, cbbc_ref)
    extx_ref[0:SUBLANES, :] = extx_ref[L:L + SUBLANES, :]
    extbc_ref[0:SUBLANES, :] = extbc_ref[L:L + SUBLANES, :]

    dt = _softplus(dt_ref[...] + dtb_ref[...])
    dtt = _softplus(dtt_ref[...] + dtbt_ref[...])
    da = dt * (-jnp.exp(a_ref[...]))
    dat = dtt * (-jnp.exp(at_ref[...]))
    r = lax.broadcasted_iota(jnp.int32, (L, L), 0)
    s_ = lax.broadcasted_iota(jnp.int32, (L, L), 1)
    causal = r >= s_
    tril = jnp.where(causal, 1.0, 0.0)
    triu = jnp.where(r <= s_, 1.0, 0.0)
    cs = jnp.dot(tril, da, precision=HIGHEST, preferred_element_type=F32)
    cst = jnp.dot(dat, triu, precision=HIGHEST, preferred_element_type=F32)
    ecs = jnp.exp(cs)
    cs_last = cs[L - 1:L, :]
    dst = jnp.exp(cs_last - cs)

    cols = jnp.concatenate([dt, ecs, dst], axis=0)
    expand = exp_ref[...]
    wide = jnp.zeros((3 * L, SSM_WIDTH), F32)
    rest = cols
    for _ in range(3):
        piece = rest.astype(BF16)
        wide = wide + jnp.dot(piece, expand, preferred_element_type=F32)
        rest = rest - piece.astype(F32)
    dt_w, ecs_w, dst_w = wide[0:L], wide[L:2 * L], wide[2 * L:3 * L]
    xdt_w = xs * dt_w
    xdt_b = xdt_w.astype(BF16)
    xend_b = (xdt_w * dst_w).astype(BF16)
    skip_w = xs * dskw_ref[...]

    for g in range(SSM_GROUPS):
        bg = bcm[:, g * N:(g + 1) * N]
        cg = bcm[:, (SSM_GROUPS + g) * N:(SSM_GROUPS + g + 1) * N]
        bgb = bg.astype(BF16)
        cgb = cg.astype(BF16)
        cb = _dot_nt(cgb, bgb)
        bgt = jnp.transpose(bg).astype(BF16)
        for hh in range(hpg):
            h = g * hpg + hh
            hp = slice(h * P, (h + 1) * P)
            seg = cs[:, h:h + 1] - cst[h:h + 1, :]
            decay = jnp.exp(jnp.where(causal, seg, NEG))
            y = jnp.dot((cb * decay).astype(BF16), xdt_b[:, hp], preferred_element_type=F32)
            st = st_ref[h]
            y = y + jnp.dot(cgb, st.astype(BF16), preferred_element_type=F32) * ecs_w[:, hp]
            y_ref[:, hp] = y + skip_w[:, hp]
            new = jnp.dot(bgt, xend_b[:, hp], preferred_element_type=F32)
            st_ref[h] = st * ecs_w[L - 1:L, hp] + new

    y = y_ref[...] * _silu(z_ref[...])
    gw = SSM_WIDTH // SSM_GROUPS
    for g in range(SSM_GROUPS):
        yg = y[:, g * gw:(g + 1) * gw]
        o_ref[:, g * gw:(g + 1) * gw] = _rms(yg, gain_ref[:, g * gw:(g + 1) * gw]).astype(o_ref.dtype)


def _pad_lanes(v):
    return jnp.zeros((1, LANES), F32).at[0, :v.shape[0]].set(v.astype(F32))


def _ssd(proj, dt, dtt, conv_w, conv_b, dt_bias, a_log, d_skip, gain, bsz, seq):
    t = bsz * seq
    nc = seq // SSD_CHUNK
    L = SSD_CHUNK
    nbc = 2 * SSM_GROUPS * SSM_STATE
    zcol = 3 * ATT_WIDTH // SSM_WIDTH
    xcol = zcol + 1
    bccol = (3 * ATT_WIDTH + 2 * SSM_WIDTH) // nbc
    dtb = _pad_lanes(dt_bias)
    alog = jnp.full((1, LANES), -jnp.inf, F32).at[0, :SSM_HEADS].set(a_log.astype(F32))
    dsk_wide = jnp.repeat(d_skip.astype(F32), SSM_HEAD_DIM).reshape(1, SSM_WIDTH)
    expand = jnp.asarray(np.arange(LANES)[:, None] == np.arange(SSM_WIDTH)[None, :] // SSM_HEAD_DIM, BF16)
    full = lambda shape: pl.BlockSpec(shape, lambda b, c: tuple(0 for _ in shape))
    return pl.pallas_call(
        _ssd_kernel,
        out_shape=jax.ShapeDtypeStruct((t, SSM_WIDTH), BF16),
        grid=(bsz, nc),
        in_specs=[pl.BlockSpec((L, SSM_WIDTH), lambda b, c: (b * nc + c, zcol)),
                  pl.BlockSpec((L, SSM_WIDTH), lambda b, c: (b * nc + c, xcol)),
                  pl.BlockSpec((L, nbc), lambda b, c: (b * nc + c, bccol)),
                  pl.BlockSpec((L, LANES), lambda b, c: (b * nc + c, 0)),
                  pl.BlockSpec((LANES, L), lambda b, c: (0, b * nc + c)),
                  full((SSM_CONV, SSM_WIDTH)), full((SSM_CONV, nbc)),
                  full((1, SSM_WIDTH)), full((1, nbc)),
                  full((1, LANES)), full((LANES, 1)), full((1, LANES)), full((LANES, 1)),
                  full((1, SSM_WIDTH)), full((1, SSM_WIDTH)), full((LANES, SSM_WIDTH))],
        out_specs=pl.BlockSpec((L, SSM_WIDTH), lambda b, c: (b * nc + c, 0)),
        scratch_shapes=[pltpu.VMEM((L + 2 * SUBLANES, SSM_WIDTH), F32),
                        pltpu.VMEM((L + 2 * SUBLANES, nbc), F32),
                        pltpu.VMEM((SSM_HEADS, SSM_STATE, SSM_HEAD_DIM), F32),
                        pltpu.VMEM((L, SSM_WIDTH), F32)],
        compiler_params=_cparams("arbitrary", "arbitrary"),
        name="ssd",
    )(proj, proj, proj, dt, dtt,
      conv_w[:, :SSM_WIDTH], conv_w[:, SSM_WIDTH:],
      conv_b[:SSM_WIDTH].reshape(1, -1), conv_b[SSM_WIDTH:].reshape(1, -1),
      dtb, dtb.reshape(LANES, 1), alog, alog.reshape(LANES, 1), dsk_wide, gain.reshape(1, -1), expand)


def _out_q_kernel(att_ref, y_ref, g_ref, x_ref, w_ref, fg_ref, wq_ref, x1_ref, pq_ref, mix_ref):
    mix_ref[:, :ATT_WIDTH] = _rms(att_ref[...], g_ref[...]).astype(BF16)
    mix_ref[:, ATT_WIDTH:] = y_ref[...]
    x1 = x_ref[...] + jnp.dot(mix_ref[...], w_ref[...], preferred_element_type=F32)
    x1_ref[...] = x1
    pq_ref[...] = jnp.dot(_rms(x1, fg_ref[...]).astype(BF16), wq_ref[...], preferred_element_type=F32)


def _out_q(att, y, gain, x, w_out, ffn_gain, wq, tm):
    t, d = x.shape
    k = w_out.shape[0]
    nq = wq.shape[1]
    row = lambda width: pl.BlockSpec((tm, width), lambda i: (i, 0))
    const = lambda shape: pl.BlockSpec(shape, lambda i: (0, 0))
    weight = lambda shape: pl.BlockSpec(shape, lambda i: (0, 0), pipeline_mode=pl.Buffered(1))
    return pl.pallas_call(
        _out_q_kernel,
        out_shape=(jax.ShapeDtypeStruct((t, d), F32), jax.ShapeDtypeStruct((t, nq), F32)),
        grid=(t // tm,),
        in_specs=[row(ATT_WIDTH), row(SSM_WIDTH), const((1, ATT_WIDTH)), row(d), weight((k, d)),
                  const((1, d)), weight((d, nq))],
        out_specs=(row(d), row(nq)),
        scratch_shapes=[pltpu.VMEM((tm, k), BF16)],
        compiler_params=_cparams("arbitrary"),
        name="out_q",
    )(att, y, gain.reshape(1, -1), x, w_out, ffn_gain.reshape(1, -1), wq)


def _peer_topk_kernel(q_ref, sk_ref, e_ref, g_ref, v_all, i_all, top_all, etop_all):
    tb = q_ref.shape[0]
    K = PEER_TOPK
    S = SUBLANES
    key_iota = lax.broadcasted_iota(jnp.int32, (PEER_NKEYS, tb), 0).astype(F32)
    sub = lax.broadcasted_iota(jnp.int32, (S, tb), 0).astype(F32)
    for h in range(PEER_HEADS):
        v_ref, i_ref, top_ref, etop_ref = v_all.at[h], i_all.at[h], top_all.at[h], etop_all.at[h]
        for c in range(2):
            col = (2 * h + c) * PEER_HALF
            sc = _dot_nt(sk_ref[h, c], q_ref[:, col:col + PEER_HALF], precision=HIGHEST)
            for i in range(K):
                m = jnp.max(sc, axis=0, keepdims=True)
                idx = jnp.min(jnp.where(sc == m, key_iota, float(PEER_NKEYS)), axis=0, keepdims=True)
                v_ref[c, i:i + 1, :] = m
                i_ref[c, i:i + 1, :] = idx
                sc = jnp.where(key_iota == idx, -jnp.inf, sc)
        v2lo, v2hi = v_ref[1, 0:S, :], v_ref[1, S:K, :]
        i2lo, i2hi = i_ref[1, 0:S, :], i_ref[1, S:K, :]
        cands, eids, flats = [], [], []
        for a in range(S):
            cands.append(v_ref[0, a:a + 1, :] + v2lo)
            eids.append(i_ref[0, a:a + 1, :] * PEER_NKEYS + i2lo)
            flats.append(a * K + sub)
        cands.append(v_ref[0, 0:1, :] + v2hi)
        eids.append(i_ref[0, 0:1, :] * PEER_NKEYS + i2hi)
        flats.append(S + sub)
        cands.append(v_ref[0, S:K, :] + v_ref[1, 0:1, :])
        eids.append(i_ref[0, S:K, :] * PEER_NKEYS + i_ref[1, 0:1, :])
        flats.append((S + sub) * K)
        for i in range(K):
            m = functools.reduce(jnp.maximum, cands)
            m = jnp.max(m, axis=0, keepdims=True)
            pos = functools.reduce(jnp.minimum,
                                   [jnp.where(cd == m, fl, float(K * K)) for cd, fl in zip(cands, flats)])
            pos = jnp.min(pos, axis=0, keepdims=True)
            hits = [fl == pos for fl in flats]
            e = functools.reduce(jnp.add, [jnp.where(ht, ei, 0.0) for ht, ei in zip(hits, eids)])
            etop_ref[i:i + 1, :] = jnp.sum(e, axis=0, keepdims=True)
            top_ref[i:i + 1, :] = m
            cands = [jnp.where(ht, -jnp.inf, cd) for ht, cd in zip(hits, cands)]
        top_s = top_ref[...]
        p = jnp.exp(top_s - top_s[0:1, :])
        g_ref[h * K:(h + 1) * K, :] = p / jnp.sum(p, axis=0, keepdims=True)
        e_ref[h * K:(h + 1) * K, :] = etop_ref[...].astype(jnp.int32)


def _peer_topk(q, subkeys, tb):
    t, d = q.shape
    return pl.pallas_call(
        _peer_topk_kernel,
        out_shape=(jax.ShapeDtypeStruct((PEER_HK, t), jnp.int32),
                   jax.ShapeDtypeStruct((PEER_HK, t), F32)),
        grid=(t // tb,),
        in_specs=[pl.BlockSpec((tb, d), lambda i: (i, 0)),
                  pl.BlockSpec(subkeys.shape, lambda i: (0, 0, 0, 0))],
        out_specs=(pl.BlockSpec((PEER_HK, tb), lambda i: (0, i)),
                   pl.BlockSpec((PEER_HK, tb), lambda i: (0, i))),
        scratch_shapes=[pltpu.VMEM((PEER_HEADS, 2, PEER_TOPK, tb), F32),
                        pltpu.VMEM((PEER_HEADS, 2, PEER_TOPK, tb), F32),
                        pltpu.VMEM((PEER_HEADS, PEER_TOPK, tb), F32),
                        pltpu.VMEM((PEER_HEADS, PEER_TOPK, tb), F32)],
        compiler_params=_cparams("arbitrary"),
        name="peer_topk",
    )(q, subkeys)


def _gelu(x):
    return 0.5 * x * (1.0 + lax.erf(x * (1.0 / math.sqrt(2.0))))


def _unpack_bf16_pair(w):
    lo = lax.bitcast_convert_type(w << 16, F32)
    hi = lax.bitcast_convert_type(w & jnp.uint32(0xFFFF0000), F32)
    return lo, hi


def _peer_experts_kernel(x_ref, gates_ref, fg_ref, og_ref, ids_hbm, uv_hbm, o_ref,
                         buf0, buf1, buf2, ids_smem, sem, ids_sem):
    i = pl.program_id(0)
    n = pl.num_programs(0)
    sub = PEER_RING_TOKENS
    nsub = x_ref.shape[0] // sub
    nw = x_ref.shape[1] // (2 * LANES)
    rows = sub * PEER_HK
    pitch = PEER_ROW_PITCH
    bufs = (buf0, buf1, buf2)
    nbuf = len(bufs)
    assert nsub == 2 and nbuf == 3, "the ring arithmetic below is written for 2 sub-blocks, 3 buffers"

    def ids_copy(step, slot):
        return pltpu.make_async_copy(ids_hbm.at[step], ids_smem.at[slot], ids_sem.at[slot])

    def issue_token(slot, u, j, dst, dst_sem):
        for k in range(PEER_HK):
            r = j * PEER_HK + k
            e = ids_smem[slot, 0, u * rows + r]
            src = uv_hbm.at[pl.ds(pl.multiple_of(e * (2 * nw), 2 * nw), 2 * nw), :]
            pltpu.make_async_copy(src, dst.at[pl.ds(r * pitch, 2 * nw), :], dst_sem).start(priority=k % 2)

    def wait_all(b):
        window = pl.ds(0, rows * 2 * nw)
        pltpu.make_async_copy(bufs[(b + 1) % nbuf].at[window, :], bufs[b].at[window, :], sem.at[b]).wait()

    def chunk(buf, j, s):
        return buf[pl.ds(j * PEER_HK * pitch + s, PEER_HK, stride=pitch), :]

    @pl.when(i == 0)
    def _():
        ids_copy(0, 0).start()
        ids_copy(jnp.minimum(1, n - 1), 2).start()
        ids_copy(0, 0).wait()
        for u in range(nsub):
            for j in range(sub):
                issue_token(0, u, j, bufs[u], sem.at[u])

    def step(p):
        nxt_slot = (p + 2) % nbuf
        ids_copy(0, nxt_slot).wait()
        ids_copy(jnp.minimum(i + 2, n - 1), (p + 1) % nbuf).start()
        x = x_ref[...]
        hx = _rms(x, fg_ref[...])
        gates = gates_ref[...]
        eye = (lax.broadcasted_iota(jnp.int32, (PEER_HK, PEER_HK), 0)
               == lax.broadcasted_iota(jnp.int32, (PEER_HK, PEER_HK), 1))
        outs = []
        for u in range(nsub):
            b = (p + u) % nbuf
            ahead = (b + 2) % nbuf
            cur = bufs[b]
            wait_all(b)
            for j in range(sub):
                t = u * sub + j
                acc = jnp.zeros((PEER_HK, LANES), F32)
                for s in range(nw):
                    lo, hi = _unpack_bf16_pair(chunk(cur, j, s))
                    acc = acc + lo * hx[t:t + 1, s * LANES:(s + 1) * LANES]
                    acc = acc + hi * hx[t:t + 1, (nw + s) * LANES:(nw + s + 1) * LANES]
                pre = jnp.sum(acc, axis=1, keepdims=True)
                gcol = jnp.sum(jnp.where(eye, gates[t:t + 1, :], 0.0), axis=1, keepdims=True)
                act = _gelu(pre) * gcol
                los, his = [], []
                for s in range(nw):
                    lo, hi = _unpack_bf16_pair(chunk(cur, j, nw + s))
                    los.append(jnp.sum(lo * act, axis=0, keepdims=True))
                    his.append(jnp.sum(hi * act, axis=0, keepdims=True))
                outs.append(jnp.concatenate(los + his, axis=1))
                issue_token(nxt_slot, u, j, bufs[ahead], sem.at[ahead])
        x2 = x + jnp.concatenate(outs, axis=0)
        o_ref[...] = _rms(x2, og_ref[...])

        @pl.when(i == n - 1)
        def _():
            wait_all((p + 2) % nbuf)
            wait_all(p)
            ids_copy(0, (p + 1) % nbuf).wait()

    for p in range(nbuf):
        pl.when(lax.rem(nsub * i, nbuf) == p)(functools.partial(step, p))


def _peer_experts(x1, eidx, gates, ffn_gain, out_gain, uv_tab):
    t, d = x1.shape
    tps = PEER_TOKENS_PER_STEP
    n = t // tps
    rows = PEER_RING_TOKENS * PEER_HK
    buf = pltpu.VMEM((rows * PEER_ROW_PITCH, LANES), jnp.uint32)
    return pl.pallas_call(
        _peer_experts_kernel,
        out_shape=jax.ShapeDtypeStruct((t, d), F32),
        grid=(n,),
        in_specs=[pl.BlockSpec((tps, d), lambda i: (i, 0)),
                  pl.BlockSpec((tps, PEER_HK), lambda i: (i, 0)),
                  pl.BlockSpec((1, d), lambda i: (0, 0)),
                  pl.BlockSpec((1, d), lambda i: (0, 0)),
                  pl.BlockSpec(memory_space=pl.ANY),
                  pl.BlockSpec(memory_space=pl.ANY)],
        out_specs=pl.BlockSpec((tps, d), lambda i: (i, 0)),
        scratch_shapes=[buf, buf, buf,
                        pltpu.SMEM((3, 1, tps * PEER_HK), jnp.int32),
                        pltpu.SemaphoreType.DMA((3,)),
                        pltpu.SemaphoreType.DMA((3,))],
        compiler_params=_cparams("arbitrary"),
        name="peer_experts",
    )(x1, gates, ffn_gain.reshape(1, d), out_gain.reshape(1, d), eidx.reshape(n, 1, tps * PEER_HK), uv_tab)


def _pack_tables_kernel(u_ref, v_ref, o_ref):
    eb, d = u_ref.shape
    nw = d // (2 * LANES)

    def bf16_bits(x):
        return lax.bitcast_convert_type(x.astype(BF16).astype(F32), jnp.uint32)

    for t, ref in enumerate((u_ref, v_ref)):
        for s in range(nw):
            lo = bf16_bits(ref[:, s * LANES:(s + 1) * LANES]) >> 16
            hi = bf16_bits(ref[:, (nw + s) * LANES:(nw + s + 1) * LANES])
            o_ref[pl.ds(t * nw + s, eb, stride=2 * nw), :] = hi | lo


def _merge_expert_tables(u_tab, v_tab):
    ne, d = u_tab.shape
    nw = d // (2 * LANES)
    eb = 256
    return pl.pallas_call(
        _pack_tables_kernel,
        out_shape=jax.ShapeDtypeStruct((ne * 2 * nw, LANES), jnp.uint32),
        grid=(ne // eb,),
        in_specs=[pl.BlockSpec((eb, d), lambda i: (i, 0)),
                  pl.BlockSpec((eb, d), lambda i: (i, 0))],
        out_specs=pl.BlockSpec((eb * 2 * nw, LANES), lambda i: (i, 0)),
        compiler_params=_cparams("arbitrary"),
        name="pack_tables",
    )(u_tab, v_tab)


def _layer(x2d, bsz, seq, attn_norm, w_in, conv_w, conv_b, dt_bias, a_log, d_skip, rel_bias,
           attn_out_norm, ssm_out_norm, w_out, ffn_norm, peer_wq, peer_subkeys, peer_u, peer_v,
           out_gain, buckets, far_bucket):
    n_main = 3 * ATT_WIDTH + 2 * SSM_WIDTH + 2 * SSM_GROUPS * SSM_STATE
    w_main = w_in[:, :n_main].astype(BF16)
    w_dt = jnp.zeros((w_in.shape[0], LANES), F32).at[:, :SSM_HEADS].set(w_in[:, n_main:])
    proj, dt, dtt = _in_proj(x2d, attn_norm, w_main, w_dt, IN_PROJ_ROWS, IN_PROJ_COLS)
    tiles = _bias_tiles(rel_bias, buckets)
    att = _moba(proj, rel_bias, tiles, far_bucket, bsz, seq)
    y = _ssd(proj, dt, dtt, conv_w, conv_b, dt_bias, a_log, d_skip, ssm_out_norm, bsz, seq)
    x1, pq = _out_q(att, y, attn_out_norm, x2d, w_out.astype(BF16), ffn_norm, peer_wq.astype(BF16),
                    SQUARE_PROJ_ROWS)
    e_t, g_t = _peer_topk(pq, peer_subkeys, LANES)
    uv = _merge_expert_tables(peer_u, peer_v)
    return _peer_experts(x1, e_t.T, g_t.T, ffn_norm, out_gain, uv)


def kernel(x, attn_norm, w_in, conv_w, conv_b, dt_bias, a_log, d_skip, rel_bias, attn_out_norm,
           ssm_out_norm, w_out, ffn_norm, peer_wq, peer_subkeys, peer_u, peer_v, final_norm):
    bsz, seq, d = x.shape
    depth = w_in.shape[0]
    assert depth == 1, "the final rmsnorm is fused into the last layer's expert kernel"
    buckets_np, far_bucket = _bucket_tiles(seq)
    buckets = jnp.asarray(buckets_np)
    out = _layer(x.reshape(bsz * seq, d), bsz, seq, attn_norm[0], w_in[0], conv_w[0], conv_b[0],
                 dt_bias[0], a_log[0], d_skip[0], rel_bias, attn_out_norm[0], ssm_out_norm[0],
                 w_out[0], ffn_norm[0], peer_wq[0], peer_subkeys[0], peer_u[0], peer_v[0],
                 final_norm, buckets, far_bucket)
    return out.reshape(bsz, seq, d)
```
